```python
import jax, jax.numpy as jnp
from jax import lax
import numpy as np

D_MODEL = 1024
BATCH = 8
SEQ = 4096
DEPTH = 1
DEC_BATCH = 4
DEC_SEQ = 8192
PAST_LEN = 128

MIX_W = D_MODEL
CONV_W = MIX_W // 2
ML_W = MIX_W - CONV_W
ML_HEADS = 4
ML_HEAD_DIM = ML_W // ML_HEADS
ML_CHUNK = 128
CONV_K = 3
IN_COLS = 3 * CONV_W + 4 * ML_W + 4 * ML_HEADS
N_MEM = 256
XA_HEADS = 4
XA_HEAD_DIM = D_MODEL // XA_HEADS
PK_HEADS = 8
PK_DIM = D_MODEL // 4
N_KEYS = 128
N_EXPERTS = N_KEYS * N_KEYS
PK_TOPK = 16
PEER_BLOCK = 128
EPS = 1e-6
M_INIT = -1e30

kernel_name = "hymba_conv_mlstm_peer_encoder"


def _rms(x, w):
    xf = x.astype(jnp.float32)
    y = xf * lax.rsqrt(jnp.mean(xf * xf, axis=-1, keepdims=True) + EPS)
    return (y * w.astype(jnp.float32)).astype(x.dtype)


def _mlstm_dir(q, k, v, ig, fg):
    Bn, H, S, dh = q.shape
    nc = S // ML_CHUNK
    L = ML_CHUNK
    q = q.reshape(Bn, H, nc, L, dh) * (dh ** -0.5)
    k = k.reshape(Bn, H, nc, L, dh)
    v = v.reshape(Bn, H, nc, L, dh)
    ig = ig.reshape(Bn, H, nc, L)
    a = jnp.cumsum(jax.nn.log_sigmoid(fg).reshape(Bn, H, nc, L), axis=-1)
    g = a[..., -1]
    w = g[..., None] - a + ig

    def step(carry, inp):
        C, n, m = carry
        g_c, w_c, k_c, v_c = inp
        m_new = jnp.maximum(g_c + m, jnp.max(w_c, axis=-1))
        decay = jnp.exp(g_c + m - m_new)
        sc = jnp.exp(w_c - m_new[..., None])
        C_new = decay[..., None, None] * C + jnp.einsum('bhl,bhlk,bhlv->bhkv', sc, k_c, v_c)
        n_new = decay[..., None] * n + jnp.einsum('bhl,bhlk->bhk', sc, k_c)
        return (C_new, n_new, m_new), (C, n, m)

    init = (jnp.zeros((Bn, H, dh, dh), jnp.float32),
            jnp.zeros((Bn, H, dh), jnp.float32),
            jnp.full((Bn, H), M_INIT, jnp.float32))
    xs = (jnp.moveaxis(g, 2, 0), jnp.moveaxis(w, 2, 0), jnp.moveaxis(k, 2, 0), jnp.moveaxis(v, 2, 0))
    _, (C_prev, n_prev, m_prev) = lax.scan(step, init, xs)
    C_prev = jnp.moveaxis(C_prev, 0, 2)
    n_prev = jnp.moveaxis(n_prev, 0, 2)
    m_prev = jnp.moveaxis(m_prev, 0, 2)

    causal = jnp.tril(jnp.ones((L, L), bool))
    dlog = jnp.where(causal, a[..., :, None] - a[..., None, :] + ig[..., None, :], -jnp.inf)
    m_inter = a + m_prev[..., None]
    m_t = jnp.maximum(m_inter, jnp.max(dlog, axis=-1))
    s = jnp.exp(dlog - m_t[..., None]) * jnp.einsum('bhctd,bhcsd->bhcts', q, k)
    e_inter = jnp.exp(m_inter - m_t)
    num = (jnp.einsum('bhcts,bhcsd->bhctd', s, v)
           + e_inter[..., None] * jnp.einsum('bhctk,bhckv->bhctv', q, C_prev))
    den = jnp.sum(s, axis=-1) + e_inter * jnp.einsum('bhctk,bhck->bhct', q, n_prev)
    h = num / jnp.maximum(jnp.abs(den), jnp.exp(-m_t))[..., None]
    return h.reshape(Bn, H, S, dh)


def _mixer(h, w_in, gate_b, conv_w, conv_b, ml_norm_w, w_out):
    Bn, S, _ = h.shape
    z = h @ w_in
    c0 = 3 * CONV_W
    idx = [CONV_W, 2 * CONV_W, c0, c0 + ML_W, c0 + 2 * ML_W, c0 + 3 * ML_W, c0 + 4 * ML_W]
    xt, bg, cg, q, k, v, o, gates = jnp.split(z, idx, axis=-1)

    u = cg * xt
    up = jnp.pad(u, ((0, 0), (1, 1), (0, 0)))
    conv = conv_w[0] * up[:, :-2] + conv_w[1] * up[:, 1:-1] + conv_w[2] * up[:, 2:] + conv_b
    y_conv = bg * conv

    def heads(t):
        return t.reshape(Bn, S, ML_HEADS, ML_HEAD_DIM).transpose(0, 2, 1, 3).astype(jnp.float32)
    qh, kh, vh = heads(q), heads(k), heads(v)
    gt = (gates + gate_b).astype(jnp.float32).reshape(Bn, S, 4, ML_HEADS).transpose(2, 0, 3, 1)
    h_fwd = _mlstm_dir(qh, kh, vh, gt[0], gt[2])
    h_bwd = _mlstm_dir(qh[:, :, ::-1], kh[:, :, ::-1], vh[:, :, ::-1],
                       gt[1][:, :, ::-1], gt[3][:, :, ::-1])[:, :, ::-1]
    hm = (h_fwd + h_bwd).transpose(0, 2, 1, 3)
    hm = _rms(hm, ml_norm_w.reshape(ML_HEADS, ML_HEAD_DIM)).reshape(Bn, S, ML_W).astype(h.dtype)
    y_ml = jax.nn.sigmoid(o) * hm

    return jnp.concatenate([y_conv, y_ml], axis=-1) @ w_out


def _cross_attn(h, memn, wq, wk, wv, wo):
    Bn, S, _ = h.shape
    q = (h @ wq).reshape(Bn, S, XA_HEADS, XA_HEAD_DIM)
    k = (memn @ wk).reshape(Bn, -1, XA_HEADS, XA_HEAD_DIM)
    v = (memn @ wv).reshape(Bn, -1, XA_HEADS, XA_HEAD_DIM)
    s = jnp.einsum('bshd,bmhd->bhsm', q, k).astype(jnp.float32) * (XA_HEAD_DIM ** -0.5)
    p = jax.nn.softmax(s, axis=-1).astype(h.dtype)
    o = jnp.einsum('bhsm,bmhd->bshd', p, v).reshape(Bn, S, D_MODEL)
    return o @ wo


def _peer(h, wq, keys1, keys2, u_tab, v_tab):
    Bn, S, _ = h.shape
    T = Bn * S
    ht = h.reshape(T, D_MODEL)
    q = (ht @ wq).reshape(T, PK_HEADS, PK_DIM)
    half = PK_DIM // 2
    s1 = jnp.einsum('thd,hnd->thn', q[..., :half], keys1)
    s2 = jnp.einsum('thd,hnd->thn', q[..., half:], keys2)
    v1, i1 = lax.top_k(s1, PK_TOPK)
    v2, i2 = lax.top_k(s2, PK_TOPK)
    cand = (v1[..., :, None] + v2[..., None, :]).reshape(T, PK_HEADS, PK_TOPK * PK_TOPK)
    cidx = (i1[..., :, None] * N_KEYS + i2[..., None, :]).reshape(T, PK_HEADS, PK_TOPK * PK_TOPK)
    sc, pos = lax.top_k(cand, PK_TOPK)
    eidx = jnp.take_along_axis(cidx, pos, axis=-1)
    gate = jax.nn.softmax(sc.astype(jnp.float32), axis=-1).astype(h.dtype)

    nb = T // PEER_BLOCK

    def block(args):
        xb, ib, gb = args
        ue = jnp.take(u_tab, ib, axis=0)
        ve = jnp.take(v_tab, ib, axis=0)
        act = jax.nn.gelu(jnp.einsum('td,thkd->thk', xb, ue))
        return jnp.einsum('thk,thkd->td', gb * act, ve)

    out = lax.map(block, (ht.reshape(nb, PEER_BLOCK, D_MODEL),
                          eidx.reshape(nb, PEER_BLOCK, PK_HEADS, PK_TOPK),
                          gate.reshape(nb, PEER_BLOCK, PK_HEADS, PK_TOPK)))
    return out.reshape(Bn, S, D_MODEL)


def _trunk(x, mem, norm_mix_w, w_in, gate_b, conv_w, conv_b, mlstm_norm_w, w_out,
           norm_xattn_w, norm_mem_w, xa_wq, xa_wk, xa_wv, xa_wo,
           norm_ffn_w, peer_wq, peer_keys1, peer_keys2, peer_u, peer_v, norm_final_w):
    for l in range(DEPTH):
        x = x + _mixer(_rms(x, norm_mix_w[l]), w_in[l], gate_b[l], conv_w[l], conv_b[l],
                       mlstm_norm_w[l], w_out[l])
        x = x + _cross_attn(_rms(x, norm_xattn_w[l]), _rms(mem, norm_mem_w[l]),
                            xa_wq[l], xa_wk[l], xa_wv[l], xa_wo[l])
        x = x + _peer(_rms(x, norm_ffn_w[l]), peer_wq[l], peer_keys1[l], peer_keys2[l],
                      peer_u[l], peer_v[l])
    return _rms(x, norm_final_w)


def setup_inputs(seed: int = 0) -> dict:
    key = jax.random.key(seed)
    ks = jax.random.split(key, 28)
    f32 = jnp.float32
    L = DEPTH

    def nrm(k, shape, scale):
        return jax.random.normal(k, shape, f32) * scale

    def gain(k, shape):
        return 1.0 + 0.02 * jax.random.normal(k, shape, f32)

    i_bias = nrm(ks[6], (L, 2 * ML_HEADS), 0.1)
    f_base = jnp.tile(jnp.linspace(3.0, 6.0, ML_HEADS, dtype=f32), 2)
    f_bias = f_base[None, :] + nrm(ks[7], (L, 2 * ML_HEADS), 0.1)
    return {
        "x_prompt": nrm(ks[0], (BATCH, SEQ, D_MODEL), 1.0),
        "x_sample": nrm(ks[1], (DEC_BATCH, DEC_SEQ, D_MODEL), 1.0),
        "mem_prompt": nrm(ks[2], (BATCH, N_MEM, D_MODEL), 1.0),
        "mem_sample": nrm(ks[3], (DEC_BATCH, N_MEM, D_MODEL), 1.0),
        "norm_mix_w": gain(ks[4], (L, D_MODEL)),
        "w_in": nrm(ks[5], (L, D_MODEL, IN_COLS), D_MODEL ** -0.5),
        "gate_b": jnp.concatenate([i_bias, f_bias], axis=-1),
        "conv_w": nrm(ks[8], (L, CONV_K, CONV_W), CONV_K ** -0.5),
        "conv_b": nrm(ks[9], (L, CONV_W), 0.02),
        "mlstm_norm_w": gain(ks[10], (L, ML_W)),
        "w_out": nrm(ks[11], (L, MIX_W, D_MODEL), MIX_W ** -0.5),
        "norm_xattn_w": gain(ks[12], (L, D_MODEL)),
        "norm_mem_w": gain(ks[13], (L, D_MODEL)),
        "xa_wq": nrm(ks[14], (L, D_MODEL, D_MODEL), D_MODEL ** -0.5),
        "xa_wk": nrm(ks[15], (L, D_MODEL, D_MODEL), D_MODEL ** -0.5),
        "xa_wv": nrm(ks[16], (L, D_MODEL, D_MODEL), D_MODEL ** -0.5),
        "xa_wo": nrm(ks[17], (L, D_MODEL, D_MODEL), D_MODEL ** -0.5),
        "norm_ffn_w": gain(ks[18], (L, D_MODEL)),
        "peer_wq": nrm(ks[19], (L, D_MODEL, PK_HEADS * PK_DIM), D_MODEL ** -0.5),
        "peer_keys1": nrm(ks[20], (L, PK_HEADS, N_KEYS, PK_DIM // 2), (PK_DIM // 2) ** -0.5),
        "peer_keys2": nrm(ks[21], (L, PK_HEADS, N_KEYS, PK_DIM // 2), (PK_DIM // 2) ** -0.5),
        "peer_u": nrm(ks[22], (L, N_EXPERTS, D_MODEL), D_MODEL ** -0.5),
        "peer_v": nrm(ks[23], (L, N_EXPERTS, D_MODEL), (PK_HEADS * PK_TOPK) ** -0.5),
        "norm_final_w": gain(ks[24], (D_MODEL,)),
    }


def reference(x_prompt, x_sample, mem_prompt, mem_sample, norm_mix_w, w_in, gate_b, conv_w, conv_b,
              mlstm_norm_w, w_out, norm_xattn_w, norm_mem_w, xa_wq, xa_wk, xa_wv, xa_wo,
              norm_ffn_w, peer_wq, peer_keys1, peer_keys2, peer_u, peer_v, norm_final_w):
    params = (norm_mix_w, w_in, gate_b, conv_w, conv_b, mlstm_norm_w, w_out,
              norm_xattn_w, norm_mem_w, xa_wq, xa_wk, xa_wv, xa_wo,
              norm_ffn_w, peer_wq, peer_keys1, peer_keys2, peer_u, peer_v, norm_final_w)
    y_prompt = _trunk(x_prompt, mem_prompt, *params)
    y_sample = _trunk(x_sample, mem_sample, *params)
    return (y_prompt, y_sample)
```

```python
import functools

import jax
import jax.numpy as jnp
from jax import lax
from jax.experimental import pallas as pl
from jax.experimental.pallas import tpu as pltpu

F32 = jnp.float32
BF16 = jnp.bfloat16
I32 = jnp.int32

EPS = 1e-6
M_INIT = -1e30
LANES = 128
SUBLANES = 8

CONV_W = 512
ML_HEADS = 4
ML_DH = 128
ML_W = ML_HEADS * ML_DH
CHUNK = 128
GATE_COLS = 4 * ML_HEADS
XA_HEADS = 4
N_KEYS = 128
PK_HEADS = 8
PK_TOPK = 16
N_SEL = PK_HEADS * PK_TOPK
VMEM_LIMIT = 56 * 1024 * 1024

ZB_XT, ZB_BG, ZB_CG, ZB_Q, ZB_K, ZB_V, ZB_O = range(7)
Z_MAIN = 7 * 512
ZB_GATES = Z_MAIN // LANES


def _cparams(*sem):
    return pltpu.CompilerParams(dimension_semantics=sem, vmem_limit_bytes=VMEM_LIMIT)


def _rms_rows(x, w):
    return x * lax.rsqrt(jnp.mean(x * x, axis=-1, keepdims=True) + EPS) * w


def _log_sigmoid(x):
    return jnp.minimum(x, 0.0) - jnp.log(1.0 + jnp.exp(-jnp.abs(x)))


def _dot(a, b):
    return jnp.dot(a.astype(BF16), b.astype(BF16), preferred_element_type=F32)


def _dot_nt(a, b):
    return lax.dot_general(a.astype(BF16), b.astype(BF16), (((1,), (1,)), ((), ())),
                           preferred_element_type=F32)


def _dot_tn(a, b):
    return lax.dot_general(a.astype(BF16), b.astype(BF16), (((0,), (0,)), ((), ())),
                           preferred_element_type=F32)


def _rms_matmul_kernel(x_ref, nw_ref, w_ref, o_ref):
    h = _rms_rows(x_ref[...], nw_ref[...])
    o_ref[...] = _dot(h, w_ref[...])


def _rms_matmul(x2d, norm_w, w_bf16, tm):
    t, d = x2d.shape
    n = w_bf16.shape[1]
    return pl.pallas_call(
        _rms_matmul_kernel,
        grid=(t // tm,),
        in_specs=[pl.BlockSpec((tm, d), lambda i: (i, 0)),
                  pl.BlockSpec((1, d), lambda i: (0, 0)),
                  pl.BlockSpec((d, n), lambda i: (0, 0))],
        out_specs=pl.BlockSpec((tm, n), lambda i: (i, 0)),
        out_shape=jax.ShapeDtypeStruct((t, n), F32),
        compiler_params=_cparams("parallel"),
        name="rms_matmul",
    )(x2d, norm_w.reshape(1, d), w_bf16)


def _mlstm_direction(d, q_ref, k_ref, v_ref, g_ref, gb_ref, h_ref, c_ref, n_ref, m_ref):
    L = CHUNK
    r = lax.broadcasted_iota(I32, (L, L), 0)
    c = lax.broadcasted_iota(I32, (L, L), 1)
    allowed = (c <= r) if d == 0 else (c >= r)
    tri = allowed.astype(F32)
    g_all = g_ref[...] + gb_ref[...]
    lf = _log_sigmoid(g_all)
    a_all = jnp.dot(tri, lf, precision=lax.Precision.HIGHEST, preferred_element_type=F32)
    a_all_t = a_all.T
    g_all_t = g_all.T
    end = L - 1 if d == 0 else 0
    for hd in range(ML_HEADS):
        ci = d * ML_HEADS + hd
        cf = 2 * ML_HEADS + d * ML_HEADS + hd
        st = d * ML_HEADS + hd
        sl = slice(hd * ML_DH, (hd + 1) * ML_DH)
        q = q_ref[:, sl] * (ML_DH ** -0.5)
        k = k_ref[:, sl]
        v = v_ref[:, sl]
        a_col = a_all[:, cf:cf + 1]
        a_row = a_all_t[cf:cf + 1, :]
        ig_col = g_all[:, ci:ci + 1]
        ig_row = g_all_t[ci:ci + 1, :]
        g_tot = a_col[end:end + 1, :]
        c_prev = c_ref[st]
        n_prev = n_ref[st]
        m_prev = m_ref[st]

        dlog = jnp.where(allowed, a_col - a_row + ig_row, -jnp.inf)
        m_inter = a_col + m_prev
        m_t = jnp.maximum(m_inter, jnp.max(dlog, axis=1, keepdims=True))
        s = jnp.exp(dlog - m_t) * _dot_nt(q, k)
        e_inter = jnp.exp(m_inter - m_t)
        num = _dot(s, v) + e_inter * _dot(q, c_prev)
        den = jnp.sum(s, axis=1, keepdims=True) + e_inter * jnp.sum(q * n_prev, axis=1, keepdims=True)
        h_ref[:, sl] = num / jnp.maximum(jnp.abs(den), jnp.exp(-m_t))

        w_col = g_tot - a_col + ig_col
        m_new = jnp.maximum(g_tot + m_prev, jnp.max(w_col, axis=0, keepdims=True))
        decay = jnp.exp(g_tot + m_prev - m_new)
        ksc = k * jnp.exp(w_col - m_new)
        c_ref[st] = decay * c_prev + _dot_tn(ksc, v)
        n_ref[st] = decay * n_prev + jnp.sum(ksc, axis=0, keepdims=True)
        m_ref[st] = m_new


def _mlstm_scan_kernel(qf, kf, vf, gf, qb, kb, vb, gbk, gb_ref, hf_ref, hb_ref, c_ref, n_ref, m_ref):
    @pl.when(pl.program_id(1) == 0)
    def _():
        c_ref[...] = jnp.zeros_like(c_ref)
        n_ref[...] = jnp.zeros_like(n_ref)
        m_ref[...] = jnp.full_like(m_ref, M_INIT)

    _mlstm_direction(0, qf, kf, vf, gf, gb_ref, hf_ref, c_ref, n_ref, m_ref)
    _mlstm_direction(1, qb, kb, vb, gbk, gb_ref, hb_ref, c_ref, n_ref, m_ref)


def _mlstm_scan(z, gate_b_pad, bn, s):
    nc = s // CHUNK
    t = bn * s

    def fwd(col):
        return lambda b, i: (b * nc + i, col)

    def bwd(col):
        return lambda b, i: (b * nc + nc - 1 - i, col)

    wide = lambda im: pl.BlockSpec((CHUNK, 512), im)
    gate = lambda im: pl.BlockSpec((CHUNK, LANES), im)
    n_state = 2 * ML_HEADS
    return pl.pallas_call(
        _mlstm_scan_kernel,
        grid=(bn, nc),
        in_specs=[wide(fwd(ZB_Q)), wide(fwd(ZB_K)), wide(fwd(ZB_V)), gate(fwd(ZB_GATES)),
                  wide(bwd(ZB_Q)), wide(bwd(ZB_K)), wide(bwd(ZB_V)), gate(bwd(ZB_GATES)),
                  pl.BlockSpec((1, LANES), lambda b, i: (0, 0))],
        out_specs=[wide(fwd(0)), wide(bwd(0))],
        out_shape=[jax.ShapeDtypeStruct((t, ML_W), F32), jax.ShapeDtypeStruct((t, ML_W), F32)],
        scratch_shapes=[pltpu.VMEM((n_state, ML_DH, ML_DH), F32),
                        pltpu.VMEM((n_state, 1, ML_DH), F32),
                        pltpu.VMEM((n_state, 1, 1), F32)],
        compiler_params=_cparams("parallel", "arbitrary"),
        name="mlstm_scan",
    )(z, z, z, z, z, z, z, z, gate_b_pad)


def _mixer_out_kernel(tiles_per_seq, x_ref, xt_ref, bg_ref, cg_ref, o_ref, hf_ref, hb_ref,
                      xt_p, cg_p, xt_n, cg_n, cw_ref, cb_ref, mw_ref, wo_ref, out_ref):
    tm = x_ref.shape[0]
    pos = pl.program_id(0) % tiles_per_seq
    u = cg_ref[...] * xt_ref[...]
    has_prev = (pos > 0).astype(F32)
    has_next = (pos < tiles_per_seq - 1).astype(F32)
    u_before = (cg_p[...] * xt_p[...])[SUBLANES - 1:SUBLANES, :] * has_prev
    u_after = (cg_n[...] * xt_n[...])[0:1, :] * has_next
    row = lax.broadcasted_iota(I32, u.shape, 0)
    u_prev = jnp.where(row == 0, u_before, pltpu.roll(u, 1, axis=0))
    u_next = jnp.where(row == tm - 1, u_after, pltpu.roll(u, tm - 1, axis=0))
    conv = cw_ref[0:1, :] * u_prev + cw_ref[1:2, :] * u + cw_ref[2:3, :] * u_next + cb_ref[...]
    y_conv = bg_ref[...] * conv

    hm = hf_ref[...] + hb_ref[...]
    og = jax.nn.sigmoid(o_ref[...])
    acc = x_ref[...] + _dot(y_conv, wo_ref[0:CONV_W, :])
    for hd in range(ML_HEADS):
        sl = slice(hd * ML_DH, (hd + 1) * ML_DH)
        y_ml = og[:, sl] * _rms_rows(hm[:, sl], mw_ref[:, sl])
        acc += _dot(y_ml, wo_ref[CONV_W + hd * ML_DH:CONV_W + (hd + 1) * ML_DH, :])
    out_ref[...] = acc


def _mixer_out(x2d, z, hf, hb, conv_w, conv_b, ml_norm_w, w_out_bf16, s, tm):
    t, d = x2d.shape
    tiles_per_seq = s // tm
    rb = tm // SUBLANES
    last_rb = t // SUBLANES - 1
    wide = lambda col: pl.BlockSpec((tm, 512), lambda i: (i, col))
    halo_prev = lambda col: pl.BlockSpec((SUBLANES, 512), lambda i: (jnp.maximum(i * rb - 1, 0), col))
    halo_next = lambda col: pl.BlockSpec((SUBLANES, 512), lambda i: (jnp.minimum((i + 1) * rb, last_rb), col))
    const = lambda shape: pl.BlockSpec(shape, lambda i: (0, 0))
    return pl.pallas_call(
        functools.partial(_mixer_out_kernel, tiles_per_seq),
        grid=(t // tm,),
        in_specs=[pl.BlockSpec((tm, d), lambda i: (i, 0)),
                  wide(ZB_XT), wide(ZB_BG), wide(ZB_CG), wide(ZB_O),
                  wide(0), wide(0),
                  halo_prev(ZB_XT), halo_prev(ZB_CG), halo_next(ZB_XT), halo_next(ZB_CG),
                  const((3, CONV_W)), const((1, CONV_W)), const((1, ML_W)), const((d, d))],
        out_specs=pl.BlockSpec((tm, d), lambda i: (i, 0)),
        out_shape=jax.ShapeDtypeStruct((t, d), F32),
        compiler_params=_cparams("parallel"),
        name="mixer_out",
    )(x2d, z, z, z, z, hf, hb, z, z, z, z, conv_w, conv_b.reshape(1, -1), ml_norm_w.reshape(1, -1), w_out_bf16)


def _xattn_kernel(x_ref, kv_ref, nw_ref, wq_ref, wo_ref, out_ref):
    x = x_ref[...]
    d = x.shape[1]
    dh = d // XA_HEADS
    q = _dot(_rms_rows(x, nw_ref[...]), wq_ref[...])
    acc = x
    for hd in range(XA_HEADS):
        sl = slice(hd * dh, (hd + 1) * dh)
        s = _dot_nt(q[:, sl], kv_ref[:, sl]) * (dh ** -0.5)
        s = s - jnp.max(s, axis=-1, keepdims=True)
        p = jnp.exp(s)
        p = p / jnp.sum(p, axis=-1, keepdims=True)
        o = _dot(p, kv_ref[:, d + hd * dh:d + (hd + 1) * dh])
        acc += _dot(o, wo_ref[sl, :])
    out_ref[...] = acc


def _xattn(x2d, kv, norm_w, wq_bf16, wo_bf16, s, n_mem, tm):
    t, d = x2d.shape
    tiles_per_seq = s // tm
    const = lambda shape: pl.BlockSpec(shape, lambda i: (0, 0))
    return pl.pallas_call(
        _xattn_kernel,
        grid=(t // tm,),
        in_specs=[pl.BlockSpec((tm, d), lambda i: (i, 0)),
                  pl.BlockSpec((n_mem, 2 * d), lambda i: (i // tiles_per_seq, 0)),
                  const((1, d)), const((d, d)), const((d, d))],
        out_specs=pl.BlockSpec((tm, d), lambda i: (i, 0)),
        out_shape=jax.ShapeDtypeStruct((t, d), F32),
        compiler_params=_cparams("parallel"),
        name="xattn",
    )(x2d, kv, norm_w.reshape(1, d), wq_bf16, wo_bf16)


def _top_rows(s, prio, payload, k):
    big = jnp.int32(2 ** 30)
    vals, pays = [], []
    for _ in range(k):
        m = jnp.max(s, axis=0, keepdims=True)
        pm = jnp.min(jnp.where(s == m, prio, big), axis=0, keepdims=True)
        sel = prio == pm
        vals.append(m)
        pays.append(pm if payload is None else jnp.sum(jnp.where(sel, payload, 0), axis=0, keepdims=True))
        s = jnp.where(sel, -jnp.inf, s)
    return vals, pays


def _peer_route_kernel(x_ref, nw_ref, wq_ref, k1_ref, k2_ref, h_ref, idx_ref, gate_ref):
    tm = x_ref.shape[0]
    k = PK_TOPK
    h = _rms_rows(x_ref[...], nw_ref[...])
    h_ref[...] = h
    q = _dot(h, wq_ref[...])
    key_iota = lax.broadcasted_iota(I32, (N_KEYS, tm), 0)
    r = lax.broadcasted_iota(I32, (80, 1), 0)
    prio = jnp.where(r < 16, r,
                     jnp.where(r < 72, (1 + ((r - 16) >> 3)) * k + ((r - 16) & 7), (r - 64) * k))
    idx_rows, gate_rows = [], []
    for hd in range(PK_HEADS):
        tops = []
        for half, kref in ((0, k1_ref), (1, k2_ref)):
            c0 = hd * 2 * N_KEYS + half * N_KEYS
            s_t = _dot_nt(kref[hd * N_KEYS:(hd + 1) * N_KEYS, :], q[:, c0:c0 + N_KEYS])
            vals, ids = _top_rows(s_t, key_iota, None, k)
            tops.append((jnp.concatenate(vals, axis=0), jnp.concatenate(ids, axis=0)))
        (v1, i1), (v2, i2) = tops
        cand_blocks = [v1[0:1] + v2]
        cidx_blocks = [i1[0:1] * N_KEYS + i2]
        for a in range(1, 8):
            cand_blocks.append(v1[a:a + 1] + v2[0:8])
            cidx_blocks.append(i1[a:a + 1] * N_KEYS + i2[0:8])
        cand_blocks.append(v1[8:16] + v2[0:1])
        cidx_blocks.append(i1[8:16] * N_KEYS + i2[0:1])
        cand = jnp.concatenate(cand_blocks, axis=0)
        cidx = jnp.concatenate(cidx_blocks, axis=0)
        sc, eidx = _top_rows(cand, prio, cidx, k)
        e = [jnp.exp(v - sc[0]) for v in sc]
        tot = functools.reduce(lambda a, b: a + b, e)
        gate_rows.extend([ek / tot for ek in e])
        idx_rows.extend(eidx)
    gate_ref[...] = jnp.concatenate(gate_rows, axis=0)
    idx_ref[...] = jnp.concatenate(idx_rows, axis=0).T


def _peer_route(x2d, norm_w, wq_bf16, keys1_bf16, keys2_bf16, tm):
    t, d = x2d.shape
    const = lambda shape: pl.BlockSpec(shape, lambda i: (0, 0))
    return pl.pallas_call(
        _peer_route_kernel,
        grid=(t // tm,),
        in_specs=[pl.BlockSpec((tm, d), lambda i: (i, 0)), const((1, d)), const(wq_bf16.shape),
                  const(keys1_bf16.shape), const(keys2_bf16.shape)],
        out_specs=[pl.BlockSpec((tm, d), lambda i: (i, 0)),
                   pl.BlockSpec((tm, N_SEL), lambda i: (i, 0)),
                   pl.BlockSpec((N_SEL, tm), lambda i: (0, i))],
        out_shape=[jax.ShapeDtypeStruct((t, d), F32),
                   jax.ShapeDtypeStruct((t, N_SEL), I32),
                   jax.ShapeDtypeStruct((N_SEL, t), F32)],
        compiler_params=_cparams("parallel"),
        name="peer_route",
    )(x2d, norm_w.reshape(1, d), wq_bf16, keys1_bf16, keys2_bf16)


def _gelu_tanh(x):
    return 0.5 * x * (1.0 + jnp.tanh(0.7978845608028654 * (x + 0.044715 * x * x * x)))


def _peer_mix_kernel(idx_hbm, uv_hbm, h_ref, gate_ref, x_ref, nw_ref, out_ref,
                     idx_smem, rows, acc_ref, idx_sem, row_sem):
    tb, d = h_ref.shape
    blk = pl.program_id(0)

    idx_copy = pltpu.make_async_copy(idx_hbm.at[pl.ds(blk * (tb * N_SEL), tb * N_SEL)], idx_smem, idx_sem)
    idx_copy.start()
    idx_copy.wait()

    def start_rows(tok, slot):
        for e in range(N_SEL):
            pltpu.make_async_copy(uv_hbm.at[pl.ds(idx_smem[tok * N_SEL + e], 1)],
                                  rows.at[slot, pl.ds(e, 1)], row_sem.at[slot]).start()

    def wait_rows(slot):
        pltpu.make_async_copy(uv_hbm.at[pl.ds(0, N_SEL)], rows.at[slot], row_sem.at[slot]).wait()

    lane = lax.broadcasted_iota(I32, (N_SEL, tb), 1)
    gate_t = gate_ref[...]

    start_rows(0, 0)

    def body(j, carry):
        slot = j % 2

        @pl.when(j + 1 < tb)
        def _():
            start_rows(j + 1, 1 - slot)

        wait_rows(slot)
        x_row = h_ref[pl.ds(j, 1), :]
        u = rows[slot, :, 0:d]
        a = jnp.sum(u * x_row, axis=1, keepdims=True)
        g = jnp.sum(jnp.where(lane == j, gate_t, 0.0), axis=1, keepdims=True)
        coef = g * _gelu_tanh(a)
        acc_ref[pl.ds(j, 1), :] = jnp.sum(coef * rows[slot, :, d:2 * d], axis=0, keepdims=True)
        return carry

    lax.fori_loop(0, tb, body, 0)
    out_ref[...] = _rms_rows(x_ref[...] + acc_ref[...], nw_ref[...])


def _peer_mix(idx_flat, uv, h, gate_t, x2d, norm_w, tb):
    t, d = x2d.shape
    return pl.pallas_call(
        _peer_mix_kernel,
        grid=(t // tb,),
        in_specs=[pl.BlockSpec(memory_space=pl.ANY),
                  pl.BlockSpec(memory_space=pl.ANY),
                  pl.BlockSpec((tb, d), lambda i: (i, 0)),
                  pl.BlockSpec((N_SEL, tb), lambda i: (0, i)),
                  pl.BlockSpec((tb, d), lambda i: (i, 0)),
                  pl.BlockSpec((1, d), lambda i: (0, 0))],
        out_specs=pl.BlockSpec((tb, d), lambda i: (i, 0)),
        out_shape=jax.ShapeDtypeStruct((t, d), F32),
        scratch_shapes=[pltpu.SMEM((tb * N_SEL,), I32),
                        pltpu.VMEM((2, N_SEL, 2 * d), F32),
                        pltpu.VMEM((tb, d), F32),
                        pltpu.SemaphoreType.DMA(()),
                        pltpu.SemaphoreType.DMA((2,))],
        compiler_params=_cparams("arbitrary"),
        name="peer_mix",
    )(idx_flat, uv, h, gate_t, x2d, norm_w.reshape(1, d))


def _trunk(x, mem, p):
    bn, s, d = x.shape
    n_mem = mem.shape[1]
    x2d = x.reshape(bn * s, d)

    z = _rms_matmul(x2d, p["norm_mix_w"], p["w_in"], tm=256)
    hf, hb = _mlstm_scan(z, p["gate_b"], bn, s)
    x1 = _mixer_out(x2d, z, hf, hb, p["conv_w"], p["conv_b"], p["mlstm_norm_w"], p["w_out"], s, tm=256)

    kv = _rms_matmul(mem.reshape(bn * n_mem, d), p["norm_mem_w"], p["xa_wkv"], tm=256)
    x2 = _xattn(x1, kv, p["norm_xattn_w"], p["xa_wq"], p["xa_wo"], s, n_mem, tm=256)

    h3, eidx, gate_t = _peer_route(x2, p["norm_ffn_w"], p["peer_wq"], p["peer_keys1"], p["peer_keys2"], tm=256)
    y = _peer_mix(eidx.reshape(-1), p["peer_uv"], h3, gate_t, x2, p["norm_final_w"], tb=128)
    return y.reshape(bn, s, d)


def kernel(x_prompt, x_sample, mem_prompt, mem_sample, norm_mix_w, w_in, gate_b, conv_w, conv_b, mlstm_norm_w, w_out, norm_xattn_w, norm_mem_w, xa_wq, xa_wk, xa_wv, xa_wo, norm_ffn_w, peer_wq, peer_keys1, peer_keys2, peer_u, peer_v, norm_final_w):
    assert w_in.shape[0] == 1, "single-layer trunk"
    d = x_prompt.shape[-1]
    gate_pad = LANES - GATE_COLS
    p = {
        "norm_mix_w": norm_mix_w[0],
        "w_in": jnp.pad(w_in[0], ((0, 0), (0, gate_pad))).astype(BF16),
        "gate_b": jnp.pad(gate_b[0], (0, gate_pad)).reshape(1, LANES),
        "conv_w": conv_w[0], "conv_b": conv_b[0], "mlstm_norm_w": mlstm_norm_w[0],
        "w_out": w_out[0].astype(BF16),
        "norm_xattn_w": norm_xattn_w[0], "norm_mem_w": norm_mem_w[0],
        "xa_wq": xa_wq[0].astype(BF16),
        "xa_wkv": jnp.concatenate([xa_wk[0], xa_wv[0]], axis=1).astype(BF16),
        "xa_wo": xa_wo[0].astype(BF16),
        "norm_ffn_w": norm_ffn_w[0],
        "peer_wq": peer_wq[0].astype(BF16),
        "peer_keys1": peer_keys1[0].reshape(PK_HEADS * N_KEYS, -1).astype(BF16),
        "peer_keys2": peer_keys2[0].reshape(PK_HEADS * N_KEYS, -1).astype(BF16),
        "peer_uv": jnp.concatenate([peer_u[0], peer_v[0]], axis=1),
        "norm_final_w": norm_final_w,
    }
    assert p["w_in"].shape[1] == Z_MAIN + LANES and d == 1024
    return (_trunk(x_prompt, mem_prompt, p), _trunk(x_sample, mem_sample, p))
```

```python
import functools

import jax
import jax.numpy as jnp
from jax import lax
from jax.experimental import pallas as pl
from jax.experimental.pallas import tpu as pltpu
from jax.experimental.pallas import tpu_sc as plsc

F32 = jnp.float32
BF16 = jnp.bfloat16
I32 = jnp.int32

EPS = 1e-6
M_INIT = -1e30
LANES = 128
SUBLANES = 8

CONV_W = 512
ML_HEADS = 4
ML_DH = 128
ML_W = ML_HEADS * ML_DH
CHUNK = 128
GATE_COLS = 4 * ML_HEADS
XA_HEADS = 4
N_KEYS = 128
PK_HEADS = 8
PK_TOPK = 16
N_SEL = PK_HEADS * PK_TOPK
VMEM_LIMIT = 56 * 1024 * 1024
SC_CORES = 2
SC_SUBCORES = 16
SC_LANES = 16
SC_TOKB = 8

ZB_XT, ZB_BG, ZB_CG, ZB_Q, ZB_K, ZB_V, ZB_O = range(7)
Z_MAIN = 7 * 512
ZB_GATES = Z_MAIN // LANES


def _cparams(*sem):
    return pltpu.CompilerParams(dimension_semantics=sem, vmem_limit_bytes=VMEM_LIMIT)


def _rms_rows(x, w):
    return x * lax.rsqrt(jnp.mean(x * x, axis=-1, keepdims=True) + EPS) * w


def _log_sigmoid(x):
    return jnp.minimum(x, 0.0) - jnp.log(1.0 + jnp.exp(-jnp.abs(x)))


def _dot(a, b):
    return jnp.dot(a.astype(BF16), b.astype(BF16), preferred_element_type=F32)


def _dot_nt(a, b):
    return lax.dot_general(a.astype(BF16), b.astype(BF16), (((1,), (1,)), ((), ())),
                           preferred_element_type=F32)


def _dot_tn(a, b):
    return lax.dot_general(a.astype(BF16), b.astype(BF16), (((0,), (0,)), ((), ())),
                           preferred_element_type=F32)


def _rms_matmul_kernel(x_ref, nw_ref, w_ref, o_ref):
    h = _rms_rows(x_ref[...], nw_ref[...])
    o_ref[...] = _dot(h, w_ref[...])


def _rms_matmul(x2d, norm_w, w_bf16, tm):
    t, d = x2d.shape
    n = w_bf16.shape[1]
    return pl.pallas_call(
        _rms_matmul_kernel,
        grid=(t // tm,),
        in_specs=[pl.BlockSpec((tm, d), lambda i: (i, 0)),
                  pl.BlockSpec((1, d), lambda i: (0, 0)),
                  pl.BlockSpec((d, n), lambda i: (0, 0))],
        out_specs=pl.BlockSpec((tm, n), lambda i: (i, 0)),
        out_shape=jax.ShapeDtypeStruct((t, n), F32),
        compiler_params=_cparams("parallel"),
        name="rms_matmul",
    )(x2d, norm_w.reshape(1, d), w_bf16)


def _mlstm_direction(d, q_ref, k_ref, v_ref, g_ref, gb_ref, h_ref, c_ref, n_ref, m_ref):
    L = CHUNK
    r = lax.broadcasted_iota(I32, (L, L), 0)
    c = lax.broadcasted_iota(I32, (L, L), 1)
    allowed = (c <= r) if d == 0 else (c >= r)
    tri = allowed.astype(F32)
    g_all = g_ref[...] + gb_ref[...]
    lf = _log_sigmoid(g_all)
    a_all = jnp.dot(tri, lf, precision=lax.Precision.HIGHEST, preferred_element_type=F32)
    a_all_t = a_all.T
    g_all_t = g_all.T
    end = L - 1 if d == 0 else 0
    for hd in range(ML_HEADS):
        ci = d * ML_HEADS + hd
        cf = 2 * ML_HEADS + d * ML_HEADS + hd
        st = d * ML_HEADS + hd
        sl = slice(hd * ML_DH, (hd + 1) * ML_DH)
        q = q_ref[:, sl] * (ML_DH ** -0.5)
        k = k_ref[:, sl]
        v = v_ref[:, sl]
        a_col = a_all[:, cf:cf + 1]
        a_row = a_all_t[cf:cf + 1, :]
        ig_col = g_all[:, ci:ci + 1]
        ig_row = g_all_t[ci:ci + 1, :]
        g_tot = a_col[end:end + 1, :]
        c_prev = c_ref[st]
        n_prev = n_ref[st]
        m_prev = m_ref[st]

        dlog = jnp.where(allowed, a_col - a_row + ig_row, -jnp.inf)
        m_inter = a_col + m_prev
        m_t = jnp.maximum(m_inter, jnp.max(dlog, axis=1, keepdims=True))
        s = jnp.exp(dlog - m_t) * _dot_nt(q, k)
        e_inter = jnp.exp(m_inter - m_t)
        num = _dot(s, v) + e_inter * _dot(q, c_prev)
        den = jnp.sum(s, axis=1, keepdims=True) + e_inter * jnp.sum(q * n_prev, axis=1, keepdims=True)
        h_ref[:, sl] = num / jnp.maximum(jnp.abs(den), jnp.exp(-m_t))

        w_col = g_tot - a_col + ig_col
        m_new = jnp.maximum(g_tot + m_prev, jnp.max(w_col, axis=0, keepdims=True))
        decay = jnp.exp(g_tot + m_prev - m_new)
        ksc = k * jnp.exp(w_col - m_new)
        c_ref[st] = decay * c_prev + _dot_tn(ksc, v)
        n_ref[st] = decay * n_prev + jnp.sum(ksc, axis=0, keepdims=True)
        m_ref[st] = m_new


def _mlstm_scan_kernel(qf, kf, vf, gf, qb, kb, vb, gbk, gb_ref, hf_ref, hb_ref, c_ref, n_ref, m_ref):
    @pl.when(pl.program_id(1) == 0)
    def _():
        c_ref[...] = jnp.zeros_like(c_ref)
        n_ref[...] = jnp.zeros_like(n_ref)
        m_ref[...] = jnp.full_like(m_ref, M_INIT)

    _mlstm_direction(0, qf, kf, vf, gf, gb_ref, hf_ref, c_ref, n_ref, m_ref)
    _mlstm_direction(1, qb, kb, vb, gbk, gb_ref, hb_ref, c_ref, n_ref, m_ref)


def _mlstm_scan(z, gate_b_pad, bn, s):
    nc = s // CHUNK
    t = bn * s

    def fwd(col):
        return lambda b, i: (b * nc + i, col)

    def bwd(col):
        return lambda b, i: (b * nc + nc - 1 - i, col)

    wide = lambda im: pl.BlockSpec((CHUNK, 512), im)
    gate = lambda im: pl.BlockSpec((CHUNK, LANES), im)
    n_state = 2 * ML_HEADS
    return pl.pallas_call(
        _mlstm_scan_kernel,
        grid=(bn, nc),
        in_specs=[wide(fwd(ZB_Q)), wide(fwd(ZB_K)), wide(fwd(ZB_V)), gate(fwd(ZB_GATES)),
                  wide(bwd(ZB_Q)), wide(bwd(ZB_K)), wide(bwd(ZB_V)), gate(bwd(ZB_GATES)),
                  pl.BlockSpec((1, LANES), lambda b, i: (0, 0))],
        out_specs=[wide(fwd(0)), wide(bwd(0))],
        out_shape=[jax.ShapeDtypeStruct((t, ML_W), F32), jax.ShapeDtypeStruct((t, ML_W), F32)],
        scratch_shapes=[pltpu.VMEM((n_state, ML_DH, ML_DH), F32),
                        pltpu.VMEM((n_state, 1, ML_DH), F32),
                        pltpu.VMEM((n_state, 1, 1), F32)],
        compiler_params=_cparams("parallel", "arbitrary"),
        name="mlstm_scan",
    )(z, z, z, z, z, z, z, z, gate_b_pad)


def _mixer_out_kernel(tiles_per_seq, x_ref, xt_ref, bg_ref, cg_ref, o_ref, hf_ref, hb_ref,
                      xt_p, cg_p, xt_n, cg_n, cw_ref, cb_ref, mw_ref, wo_ref, out_ref):
    tm = x_ref.shape[0]
    pos = pl.program_id(0) % tiles_per_seq
    u = cg_ref[...] * xt_ref[...]
    has_prev = (pos > 0).astype(F32)
    has_next = (pos < tiles_per_seq - 1).astype(F32)
    u_before = (cg_p[...] * xt_p[...])[SUBLANES - 1:SUBLANES, :] * has_prev
    u_after = (cg_n[...] * xt_n[...])[0:1, :] * has_next
    row = lax.broadcasted_iota(I32, u.shape, 0)
    u_prev = jnp.where(row == 0, u_before, pltpu.roll(u, 1, axis=0))
    u_next = jnp.where(row == tm - 1, u_after, pltpu.roll(u, tm - 1, axis=0))
    conv = cw_ref[0:1, :] * u_prev + cw_ref[1:2, :] * u + cw_ref[2:3, :] * u_next + cb_ref[...]
    y_conv = bg_ref[...] * conv

    hm = hf_ref[...] + hb_ref[...]
    og = jax.nn.sigmoid(o_ref[...])
    acc = x_ref[...] + _dot(y_conv, wo_ref[0:CONV_W, :])
    for hd in range(ML_HEADS):
        sl = slice(hd * ML_DH, (hd + 1) * ML_DH)
        y_ml = og[:, sl] * _rms_rows(hm[:, sl], mw_ref[:, sl])
        acc += _dot(y_ml, wo_ref[CONV_W + hd * ML_DH:CONV_W + (hd + 1) * ML_DH, :])
    out_ref[...] = acc


def _mixer_out(x2d, z, hf, hb, conv_w, conv_b, ml_norm_w, w_out_bf16, s, tm):
    t, d = x2d.shape
    tiles_per_seq = s // tm
    rb = tm // SUBLANES
    last_rb = t // SUBLANES - 1
    wide = lambda col: pl.BlockSpec((tm, 512), lambda i: (i, col))
    halo_prev = lambda col: pl.BlockSpec((SUBLANES, 512), lambda i: (jnp.maximum(i * rb - 1, 0), col))
    halo_next = lambda col: pl.BlockSpec((SUBLANES, 512), lambda i: (jnp.minimum((i + 1) * rb, last_rb), col))
    const = lambda shape: pl.BlockSpec(shape, lambda i: (0, 0))
    return pl.pallas_call(
        functools.partial(_mixer_out_kernel, tiles_per_seq),
        grid=(t // tm,),
        in_specs=[pl.BlockSpec((tm, d), lambda i: (i, 0)),
                  wide(ZB_XT), wide(ZB_BG), wide(ZB_CG), wide(ZB_O),
                  wide(0), wide(0),
                  halo_prev(ZB_XT), halo_prev(ZB_CG), halo_next(ZB_XT), halo_next(ZB_CG),
                  const((3, CONV_W)), const((1, CONV_W)), const((1, ML_W)), const((d, d))],
        out_specs=pl.BlockSpec((tm, d), lambda i: (i, 0)),
        out_shape=jax.ShapeDtypeStruct((t, d), F32),
        compiler_params=_cparams("parallel"),
        name="mixer_out",
    )(x2d, z, z, z, z, hf, hb, z, z, z, z, conv_w, conv_b.reshape(1, -1), ml_norm_w.reshape(1, -1), w_out_bf16)


def _xattn_kernel(x_ref, kv_ref, nw_ref, wq_ref, wo_ref, out_ref):
    x = x_ref[...]
    d = x.shape[1]
    dh = d // XA_HEADS
    q = _dot(_rms_rows(x, nw_ref[...]), wq_ref[...])
    acc = x
    for hd in range(XA_HEADS):
        sl = slice(hd * dh, (hd + 1) * dh)
        s = _dot_nt(q[:, sl], kv_ref[:, sl]) * (dh ** -0.5)
        s = s - jnp.max(s, axis=-1, keepdims=True)
        p = jnp.exp(s)
        p = p / jnp.sum(p, axis=-1, keepdims=True)
        o = _dot(p, kv_ref[:, d + hd * dh:d + (hd + 1) * dh])
        acc += _dot(o, wo_ref[sl, :])
    out_ref[...] = acc


def _xattn(x2d, kv, norm_w, wq_bf16, wo_bf16, s, n_mem, tm):
    t, d = x2d.shape
    tiles_per_seq = s // tm
    const = lambda shape: pl.BlockSpec(shape, lambda i: (0, 0))
    return pl.pallas_call(
        _xattn_kernel,
        grid=(t // tm,),
        in_specs=[pl.BlockSpec((tm, d), lambda i: (i, 0)),
                  pl.BlockSpec((n_mem, 2 * d), lambda i: (i // tiles_per_seq, 0)),
                  const((1, d)), const((d, d)), const((d, d))],
        out_specs=pl.BlockSpec((tm, d), lambda i: (i, 0)),
        out_shape=jax.ShapeDtypeStruct((t, d), F32),
        compiler_params=_cparams("parallel"),
        name="xattn",
    )(x2d, kv, norm_w.reshape(1, d), wq_bf16, wo_bf16)


def _top_rows(s, prio, payload, k):
    big = jnp.int32(2 ** 30)
    vals, pays = [], []
    for _ in range(k):
        m = jnp.max(s, axis=0, keepdims=True)
        pm = jnp.min(jnp.where(s == m, prio, big), axis=0, keepdims=True)
        sel = prio == pm
        vals.append(m)
        pays.append(pm if payload is None else jnp.sum(jnp.where(sel, payload, 0), axis=0, keepdims=True))
        s = jnp.where(sel, -jnp.inf, s)
    return vals, pays


def _peer_route_kernel(x_ref, nw_ref, wq_ref, k1_ref, k2_ref, h_ref, idx_ref, gate_ref):
    tm = x_ref.shape[0]
    k = PK_TOPK
    h = _rms_rows(x_ref[...], nw_ref[...])
    h_ref[...] = h
    q = _dot(h, wq_ref[...])
    key_iota = lax.broadcasted_iota(I32, (N_KEYS, tm), 0)
    r = lax.broadcasted_iota(I32, (80, 1), 0)
    prio = jnp.where(r < 16, r,
                     jnp.where(r < 72, (1 + ((r - 16) >> 3)) * k + ((r - 16) & 7), (r - 64) * k))
    idx_rows, gate_rows = [], []
    for hd in range(PK_HEADS):
        tops = []
        for half, kref in ((0, k1_ref), (1, k2_ref)):
            c0 = hd * 2 * N_KEYS + half * N_KEYS
            s_t = _dot_nt(kref[hd * N_KEYS:(hd + 1) * N_KEYS, :], q[:, c0:c0 + N_KEYS])
            vals, ids = _top_rows(s_t, key_iota, None, k)
            tops.append((jnp.concatenate(vals, axis=0), jnp.concatenate(ids, axis=0)))
        (v1, i1), (v2, i2) = tops
        cand_blocks = [v1[0:1] + v2]
        cidx_blocks = [i1[0:1] * N_KEYS + i2]
        for a in range(1, 8):
            cand_blocks.append(v1[a:a + 1] + v2[0:8])
            cidx_blocks.append(i1[a:a + 1] * N_KEYS + i2[0:8])
        cand_blocks.append(v1[8:16] + v2[0:1])
        cidx_blocks.append(i1[8:16] * N_KEYS + i2[0:1])
        cand = jnp.concatenate(cand_blocks, axis=0)
        cidx = jnp.concatenate(cidx_blocks, axis=0)
        sc, eidx = _top_rows(cand, prio, cidx, k)
        e = [jnp.exp(v - sc[0]) for v in sc]
        tot = functools.reduce(lambda a, b: a + b, e)
        gate_rows.extend([ek / tot for ek in e])
        idx_rows.extend(eidx)
    gate_ref[...] = jnp.concatenate(gate_rows, axis=0).T
    idx_ref[...] = jnp.concatenate(idx_rows, axis=0).T


def _peer_route(x2d, norm_w, wq_bf16, keys1_bf16, keys2_bf16, tm):
    t, d = x2d.shape
    const = lambda shape: pl.BlockSpec(shape, lambda i: (0, 0))
    return pl.pallas_call(
        _peer_route_kernel,
        grid=(t // tm,),
        in_specs=[pl.BlockSpec((tm, d), lambda i: (i, 0)), const((1, d)), const(wq_bf16.shape),
                  const(keys1_bf16.shape), const(keys2_bf16.shape)],
        out_specs=[pl.BlockSpec((tm, d), lambda i: (i, 0)),
                   pl.BlockSpec((tm, N_SEL), lambda i: (i, 0)),
                   pl.BlockSpec((tm, N_SEL), lambda i: (i, 0))],
        out_shape=[jax.ShapeDtypeStruct((t, d), F32),
                   jax.ShapeDtypeStruct((t, N_SEL), I32),
                   jax.ShapeDtypeStruct((t, N_SEL), F32)],
        compiler_params=_cparams("parallel"),
        name="peer_route",
    )(x2d, norm_w.reshape(1, d), wq_bf16, keys1_bf16, keys2_bf16)


def _peer_sc_body(idx_hbm, gate_hbm, h_hbm, u_hbm, v_hbm, out_hbm,
                  idx_v, gate_v, x_v, out_v, u_buf, v_buf, tmp, cvec, sem_u, sem_v):
    t_total, d = h_hbm.shape
    per_worker = t_total // (SC_CORES * SC_SUBCORES)
    wid = lax.axis_index("s") * SC_CORES + lax.axis_index("c")
    nchunk = d // SC_LANES
    lanes = lax.iota(I32, SC_LANES)
    zero = jnp.zeros((SC_LANES,), F32)

    def copies(t, hd, slot):
        ids = idx_v.at[t, pl.ds(hd * PK_TOPK, PK_TOPK)]
        return (pltpu.make_async_copy(u_hbm.at[ids], u_buf.at[slot], sem_u.at[slot]),
                pltpu.make_async_copy(v_hbm.at[ids], v_buf.at[slot], sem_v.at[slot]))

    def start(t, hd, slot):
        for cp in copies(t, hd, slot):
            cp.start()

    def wait(t, hd, slot):
        for cp in copies(t, hd, slot):
            cp.wait()

    def compute(t, hd, slot):
        def ubody(c, accs):
            sl = pl.ds(c * SC_LANES, SC_LANES)
            xv = x_v[t, sl]
            return tuple(accs[e] + u_buf[slot, e, sl] * xv for e in range(PK_TOPK))

        accs = lax.fori_loop(0, nchunk, ubody, (zero,) * PK_TOPK)
        for e in range(PK_TOPK):
            tmp[e, :] = accs[e]
        a = zero
        for l in range(SC_LANES):
            a = a + plsc.load_gather(tmp, [lanes, (lanes + l) & (SC_LANES - 1)])
        z = 0.7978845608028654 * (a + 0.044715 * a * a * a)
        e2 = jnp.exp(-2.0 * jnp.abs(z))
        th = (1.0 - e2) / (1.0 + e2)
        th = jnp.where(z < 0.0, -th, th)
        cvec[pl.ds(SC_LANES, SC_LANES)] = gate_v[t, pl.ds(hd * PK_TOPK, PK_TOPK)] * (0.5 * a * (1.0 + th))
        cb = [plsc.load_gather(cvec, [jnp.full((SC_LANES,), SC_LANES + e, I32)]) for e in range(PK_TOPK)]

        def vbody(c, carry):
            sl = pl.ds(c * SC_LANES, SC_LANES)
            o = zero if hd == 0 else out_v[t, sl]
            for e in range(PK_TOPK):
                o = o + cb[e] * v_buf[slot, e, sl]
            out_v[t, sl] = o
            return carry

        lax.fori_loop(0, nchunk, vbody, 0)

    def block(b, carry):
        base = wid * per_worker + b * SC_TOKB
        pltpu.sync_copy(idx_hbm.at[pl.ds(base, SC_TOKB)], idx_v)
        pltpu.sync_copy(gate_hbm.at[pl.ds(base, SC_TOKB)], gate_v)
        pltpu.sync_copy(h_hbm.at[pl.ds(base, SC_TOKB)], x_v)
        start(0, 0, 0)

        def token(t, c2):
            for hd in range(PK_HEADS):
                slot = hd % 2
                if hd + 1 < PK_HEADS:
                    start(t, hd + 1, 1 - slot)
                else:
                    @pl.when(t + 1 < SC_TOKB)
                    def _():
                        start(t + 1, 0, 1 - slot)
                wait(t, hd, slot)
                compute(t, hd, slot)
            return c2

        lax.fori_loop(0, SC_TOKB, token, 0)
        pltpu.sync_copy(out_v, out_hbm.at[pl.ds(base, SC_TOKB)])
        return carry

    lax.fori_loop(0, per_worker // SC_TOKB, block, 0)


def _peer_sc(idx, gate, h, u_tab, v_tab):
    t, d = h.shape
    assert t % (SC_CORES * SC_SUBCORES * SC_TOKB) == 0 and d % SC_LANES == 0
    mesh = plsc.VectorSubcoreMesh(core_axis_name="c", subcore_axis_name="s",
                                  num_cores=SC_CORES, num_subcores=SC_SUBCORES)
    return pl.kernel(
        _peer_sc_body,
        out_type=jax.ShapeDtypeStruct((t, d), F32),
        mesh=mesh,
        scratch_types=[
            pltpu.VMEM((SC_TOKB, N_SEL), I32),
            pltpu.VMEM((SC_TOKB, N_SEL), F32),
            pltpu.VMEM((SC_TOKB, d), F32),
            pltpu.VMEM((SC_TOKB, d), F32),
            pltpu.VMEM((2, PK_TOPK, d), F32),
            pltpu.VMEM((2, PK_TOPK, d), F32),
            pltpu.VMEM((PK_TOPK, SC_LANES), F32),
            pltpu.VMEM((2 * SC_LANES,), F32),
            pltpu.SemaphoreType.DMA((2,)),
            pltpu.SemaphoreType.DMA((2,)),
        ],
        compiler_params=pltpu.CompilerParams(needs_layout_passes=False),
        name="peer_sc",
    )(idx, gate, h, u_tab, v_tab)


def _final_kernel(x_ref, p_ref, nw_ref, out_ref):
    out_ref[...] = _rms_rows(x_ref[...] + p_ref[...], nw_ref[...])


def _final_norm(x2d, peer_out, norm_w, tm):
    t, d = x2d.shape
    row = pl.BlockSpec((tm, d), lambda i: (i, 0))
    return pl.pallas_call(
        _final_kernel,
        grid=(t // tm,),
        in_specs=[row, row, pl.BlockSpec((1, d), lambda i: (0, 0))],
        out_specs=row,
        out_shape=jax.ShapeDtypeStruct((t, d), F32),
        compiler_params=_cparams("parallel"),
        name="final_norm",
    )(x2d, peer_out, norm_w.reshape(1, d))


def _trunk(x, mem, p):
    bn, s, d = x.shape
    n_mem = mem.shape[1]
    x2d = x.reshape(bn * s, d)

    z = _rms_matmul(x2d, p["norm_mix_w"], p["w_in"], tm=256)
    hf, hb = _mlstm_scan(z, p["gate_b"], bn, s)
    x1 = _mixer_out(x2d, z, hf, hb, p["conv_w"], p["conv_b"], p["mlstm_norm_w"], p["w_out"], s, tm=256)

    kv = _rms_matmul(mem.reshape(bn * n_mem, d), p["norm_mem_w"], p["xa_wkv"], tm=256)
    x2 = _xattn(x1, kv, p["norm_xattn_w"], p["xa_wq"], p["xa_wo"], s, n_mem, tm=256)

    h3, eidx, gate = _peer_route(x2, p["norm_ffn_w"], p["peer_wq"], p["peer_keys1"], p["peer_keys2"], tm=256)
    peer_out = _peer_sc(eidx, gate, h3, p["peer_u"], p["peer_v"])
    y = _final_norm(x2, peer_out, p["norm_final_w"], tm=512)
    return y.reshape(bn, s, d)


def kernel(x_prompt, x_sample, mem_prompt, mem_sample, norm_mix_w, w_in, gate_b, conv_w, conv_b, mlstm_norm_w, w_out, norm_xattn_w, norm_mem_w, xa_wq, xa_wk, xa_wv, xa_wo, norm_ffn_w, peer_wq, peer_keys1, peer_keys2, peer_u, peer_v, norm_final_w):
    assert w_in.shape[0] == 1, "single-layer trunk"
    d = x_prompt.shape[-1]
    gate_pad = LANES - GATE_COLS
    p = {
        "norm_mix_w": norm_mix_w[0],
        "w_in": jnp.pad(w_in[0], ((0, 0), (0, gate_pad))).astype(BF16),
        "gate_b": jnp.pad(gate_b[0], (0, gate_pad)).reshape(1, LANES),
        "conv_w": conv_w[0], "conv_b": conv_b[0], "mlstm_norm_w": mlstm_norm_w[0],
        "w_out": w_out[0].astype(BF16),
        "norm_xattn_w": norm_xattn_w[0], "norm_mem_w": norm_mem_w[0],
        "xa_wq": xa_wq[0].astype(BF16),
        "xa_wkv": jnp.concatenate([xa_wk[0], xa_wv[0]], axis=1).astype(BF16),
        "xa_wo": xa_wo[0].astype(BF16),
        "norm_ffn_w": norm_ffn_w[0],
        "peer_wq": peer_wq[0].astype(BF16),
        "peer_keys1": peer_keys1[0].reshape(PK_HEADS * N_KEYS, -1).astype(BF16),
        "peer_keys2": peer_keys2[0].reshape(PK_HEADS * N_KEYS, -1).astype(BF16),
        "peer_u": peer_u[0], "peer_v": peer_v[0],
        "norm_final_w": norm_final_w,
    }
    assert p["w_in"].shape[1] == Z_MAIN + LANES and d == 1024
    return (_trunk(x_prompt, mem_prompt, p), _trunk(x_sample, mem_sample, p))
```

```python
import functools

import jax
import jax.numpy as jnp
from jax import lax
from jax.experimental import pallas as pl
from jax.experimental.pallas import tpu as pltpu
from jax.experimental.pallas import tpu_sc as plsc

F32 = jnp.float32
BF16 = jnp.bfloat16
I32 = jnp.int32

EPS = 1e-6
M_INIT = -1e30
LANES = 128
SUBLANES = 8

CONV_W = 512
ML_HEADS = 4
ML_DH = 128
ML_W = ML_HEADS * ML_DH
CHUNK = 128
GATE_COLS = 4 * ML_HEADS
XA_HEADS = 4
N_KEYS = 128
PK_HEADS = 8
PK_TOPK = 16
N_SEL = PK_HEADS * PK_TOPK
VMEM_LIMIT = 56 * 1024 * 1024
SC_CORES = 2
SC_SUBCORES = 16
SC_LANES = 16
SC_TOKB = 4
SC_SLOTS = 3
SC_AHEAD = 2
PIECE_TOKENS = 8192

ZB_XT, ZB_BG, ZB_CG, ZB_Q, ZB_K, ZB_V, ZB_O = range(7)
Z_MAIN = 7 * 512
ZB_GATES = Z_MAIN // LANES


def _cparams(*sem):
    return pltpu.CompilerParams(dimension_semantics=sem, vmem_limit_bytes=VMEM_LIMIT)


def _rms_rows(x, w):
    return x * lax.rsqrt(jnp.mean(x * x, axis=-1, keepdims=True) + EPS) * w


def _log_sigmoid(x):
    return jnp.minimum(x, 0.0) - jnp.log(1.0 + jnp.exp(-jnp.abs(x)))


def _dot(a, b):
    return jnp.dot(a.astype(BF16), b.astype(BF16), preferred_element_type=F32)


def _dot_nt(a, b):
    return lax.dot_general(a.astype(BF16), b.astype(BF16), (((1,), (1,)), ((), ())),
                           preferred_element_type=F32)


def _dot_tn(a, b):
    return lax.dot_general(a.astype(BF16), b.astype(BF16), (((0,), (0,)), ((), ())),
                           preferred_element_type=F32)


def _rms_matmul_kernel(x_ref, nw_ref, w_ref, o_ref):
    h = _rms_rows(x_ref[...], nw_ref[...])
    o_ref[...] = _dot(h, w_ref[...])


def _rms_matmul(x2d, norm_w, w_bf16, tm):
    t, d = x2d.shape
    n = w_bf16.shape[1]
    return pl.pallas_call(
        _rms_matmul_kernel,
        grid=(t // tm,),
        in_specs=[pl.BlockSpec((tm, d), lambda i: (i, 0)),
                  pl.BlockSpec((1, d), lambda i: (0, 0)),
                  pl.BlockSpec((d, n), lambda i: (0, 0))],
        out_specs=pl.BlockSpec((tm, n), lambda i: (i, 0)),
        out_shape=jax.ShapeDtypeStruct((t, n), F32),
        compiler_params=_cparams("parallel"),
        name="rms_matmul",
    )(x2d, norm_w.reshape(1, d), w_bf16)


def _mlstm_direction(d, q_ref, k_ref, v_ref, g_ref, gb_ref, h_ref, c_ref, n_ref, m_ref):
    L = CHUNK
    r = lax.broadcasted_iota(I32, (L, L), 0)
    c = lax.broadcasted_iota(I32, (L, L), 1)
    allowed = (c <= r) if d == 0 else (c >= r)
    tri = allowed.astype(F32)
    g_all = g_ref[...] + gb_ref[...]
    lf = _log_sigmoid(g_all)
    a_all = jnp.dot(tri, lf, precision=lax.Precision.HIGHEST, preferred_element_type=F32)
    a_all_t = a_all.T
    g_all_t = g_all.T
    end = L - 1 if d == 0 else 0
    for hd in range(ML_HEADS):
        ci = d * ML_HEADS + hd
        cf = 2 * ML_HEADS + d * ML_HEADS + hd
        st = d * ML_HEADS + hd
        sl = slice(hd * ML_DH, (hd + 1) * ML_DH)
        q = q_ref[:, sl] * (ML_DH ** -0.5)
        k = k_ref[:, sl]
        v = v_ref[:, sl]
        a_col = a_all[:, cf:cf + 1]
        a_row = a_all_t[cf:cf + 1, :]
        ig_col = g_all[:, ci:ci + 1]
        ig_row = g_all_t[ci:ci + 1, :]
        g_tot = a_col[end:end + 1, :]
        c_prev = c_ref[st]
        n_prev = n_ref[st]
        m_prev = m_ref[st]

        dlog = jnp.where(allowed, a_col - a_row + ig_row, -jnp.inf)
        m_inter = a_col + m_prev
        m_t = jnp.maximum(m_inter, jnp.max(dlog, axis=1, keepdims=True))
        s = jnp.exp(dlog - m_t) * _dot_nt(q, k)
        e_inter = jnp.exp(m_inter - m_t)
        num = _dot(s, v) + e_inter * _dot(q, c_prev)
        den = jnp.sum(s, axis=1, keepdims=True) + e_inter * jnp.sum(q * n_prev, axis=1, keepdims=True)
        h_ref[:, sl] = num / jnp.maximum(jnp.abs(den), jnp.exp(-m_t))

        w_col = g_tot - a_col + ig_col
        m_new = jnp.maximum(g_tot + m_prev, jnp.max(w_col, axis=0, keepdims=True))
        decay = jnp.exp(g_tot + m_prev - m_new)
        ksc = k * jnp.exp(w_col - m_new)
        c_ref[st] = decay * c_prev + _dot_tn(ksc, v)
        n_ref[st] = decay * n_prev + jnp.sum(ksc, axis=0, keepdims=True)
        m_ref[st] = m_new


def _mlstm_scan_kernel(qf, kf, vf, gf, qb, kb, vb, gbk, gb_ref, hf_ref, hb_ref, c_ref, n_ref, m_ref):
    @pl.when(pl.program_id(1) == 0)
    def _():
        c_ref[...] = jnp.zeros_like(c_ref)
        n_ref[...] = jnp.zeros_like(n_ref)
        m_ref[...] = jnp.full_like(m_ref, M_INIT)

    _mlstm_direction(0, qf, kf, vf, gf, gb_ref, hf_ref, c_ref, n_ref, m_ref)
    _mlstm_direction(1, qb, kb, vb, gbk, gb_ref, hb_ref, c_ref, n_ref, m_ref)


def _mlstm_scan(z, gate_b_pad, bn, s):
    nc = s // CHUNK
    t = bn * s

    def fwd(col):
        return lambda b, i: (b * nc + i, col)

    def bwd(col):
        return lambda b, i: (b * nc + nc - 1 - i, col)

    wide = lambda im: pl.BlockSpec((CHUNK, 512), im)
    gate = lambda im: pl.BlockSpec((CHUNK, LANES), im)
    n_state = 2 * ML_HEADS
    return pl.pallas_call(
        _mlstm_scan_kernel,
        grid=(bn, nc),
        in_specs=[wide(fwd(ZB_Q)), wide(fwd(ZB_K)), wide(fwd(ZB_V)), gate(fwd(ZB_GATES)),
                  wide(bwd(ZB_Q)), wide(bwd(ZB_K)), wide(bwd(ZB_V)), gate(bwd(ZB_GATES)),
                  pl.BlockSpec((1, LANES), lambda b, i: (0, 0))],
        out_specs=[wide(fwd(0)), wide(bwd(0))],
        out_shape=[jax.ShapeDtypeStruct((t, ML_W), F32), jax.ShapeDtypeStruct((t, ML_W), F32)],
        scratch_shapes=[pltpu.VMEM((n_state, ML_DH, ML_DH), F32),
                        pltpu.VMEM((n_state, 1, ML_DH), F32),
                        pltpu.VMEM((n_state, 1, 1), F32)],
        compiler_params=_cparams("parallel", "arbitrary"),
        name="mlstm_scan",
    )(z, z, z, z, z, z, z, z, gate_b_pad)


def _mixer_out_kernel(tiles_per_seq, x_ref, xt_ref, bg_ref, cg_ref, o_ref, hf_ref, hb_ref,
                      xt_p, cg_p, xt_n, cg_n, cw_ref, cb_ref, mw_ref, wo_ref, out_ref):
    tm = x_ref.shape[0]
    pos = pl.program_id(0) % tiles_per_seq
    u = cg_ref[...] * xt_ref[...]
    has_prev = (pos > 0).astype(F32)
    has_next = (pos < tiles_per_seq - 1).astype(F32)
    u_before = (cg_p[...] * xt_p[...])[SUBLANES - 1:SUBLANES, :] * has_prev
    u_after = (cg_n[...] * xt_n[...])[0:1, :] * has_next
    row = lax.broadcasted_iota(I32, u.shape, 0)
    u_prev = jnp.where(row == 0, u_before, pltpu.roll(u, 1, axis=0))
    u_next = jnp.where(row == tm - 1, u_after, pltpu.roll(u, tm - 1, axis=0))
    conv = cw_ref[0:1, :] * u_prev + cw_ref[1:2, :] * u + cw_ref[2:3, :] * u_next + cb_ref[...]
    y_conv = bg_ref[...] * conv

    hm = hf_ref[...] + hb_ref[...]
    og = jax.nn.sigmoid(o_ref[...])
    acc = x_ref[...] + _dot(y_conv, wo_ref[0:CONV_W, :])
    for hd in range(ML_HEADS):
        sl = slice(hd * ML_DH, (hd + 1) * ML_DH)
        y_ml = og[:, sl] * _rms_rows(hm[:, sl], mw_ref[:, sl])
        acc += _dot(y_ml, wo_ref[CONV_W + hd * ML_DH:CONV_W + (hd + 1) * ML_DH, :])
    out_ref[...] = acc


def _mixer_out(x2d, z, hf, hb, conv_w, conv_b, ml_norm_w, w_out_bf16, s, tm):
    t, d = x2d.shape
    tiles_per_seq = s // tm
    rb = tm // SUBLANES
    last_rb = t // SUBLANES - 1
    wide = lambda col: pl.BlockSpec((tm, 512), lambda i: (i, col))
    halo_prev = lambda col: pl.BlockSpec((SUBLANES, 512), lambda i: (jnp.maximum(i * rb - 1, 0), col))
    halo_next = lambda col: pl.BlockSpec((SUBLANES, 512), lambda i: (jnp.minimum((i + 1) * rb, last_rb), col))
    const = lambda shape: pl.BlockSpec(shape, lambda i: (0, 0))
    return pl.pallas_call(
        functools.partial(_mixer_out_kernel, tiles_per_seq),
        grid=(t // tm,),
        in_specs=[pl.BlockSpec((tm, d), lambda i: (i, 0)),
                  wide(ZB_XT), wide(ZB_BG), wide(ZB_CG), wide(ZB_O),
                  wide(0), wide(0),
                  halo_prev(ZB_XT), halo_prev(ZB_CG), halo_next(ZB_XT), halo_next(ZB_CG),
                  const((3, CONV_W)), const((1, CONV_W)), const((1, ML_W)), const((d, d))],
        out_specs=pl.BlockSpec((tm, d), lambda i: (i, 0)),
        out_shape=jax.ShapeDtypeStruct((t, d), F32),
        compiler_params=_cparams("parallel"),
        name="mixer_out",
    )(x2d, z, z, z, z, hf, hb, z, z, z, z, conv_w, conv_b.reshape(1, -1), ml_norm_w.reshape(1, -1), w_out_bf16)


def _xattn_kernel(x_ref, kv_ref, nw_ref, wq_ref, wo_ref, out_ref):
    x = x_ref[...]
    d = x.shape[1]
    dh = d // XA_HEADS
    q = _dot(_rms_rows(x, nw_ref[...]), wq_ref[...])
    acc = x
    for hd in range(XA_HEADS):
        sl = slice(hd * dh, (hd + 1) * dh)
        s = _dot_nt(q[:, sl], kv_ref[:, sl]) * (dh ** -0.5)
        s = s - jnp.max(s, axis=-1, keepdims=True)
        p = jnp.exp(s)
        p = p / jnp.sum(p, axis=-1, keepdims=True)
        o = _dot(p, kv_ref[:, d + hd * dh:d + (hd + 1) * dh])
        acc += _dot(o, wo_ref[sl, :])
    out_ref[...] = acc


def _xattn(x2d, kv, norm_w, wq_bf16, wo_bf16, s, n_mem, tm):
    t, d = x2d.shape
    tiles_per_seq = s // tm
    const = lambda shape: pl.BlockSpec(shape, lambda i: (0, 0))
    return pl.pallas_call(
        _xattn_kernel,
        grid=(t // tm,),
        in_specs=[pl.BlockSpec((tm, d), lambda i: (i, 0)),
                  pl.BlockSpec((n_mem, 2 * d), lambda i: (i // tiles_per_seq, 0)),
                  const((1, d)), const((d, d)), const((d, d))],
        out_specs=pl.BlockSpec((tm, d), lambda i: (i, 0)),
        out_shape=jax.ShapeDtypeStruct((t, d), F32),
        compiler_params=_cparams("parallel"),
        name="xattn",
    )(x2d, kv, norm_w.reshape(1, d), wq_bf16, wo_bf16)


def _top_rows(s, prio, payload, k):
    big = jnp.int32(2 ** 30)
    vals, pays = [], []
    for _ in range(k):
        m = jnp.max(s, axis=0, keepdims=True)
        pm = jnp.min(jnp.where(s == m, prio, big), axis=0, keepdims=True)
        sel = prio == pm
        vals.append(m)
        pays.append(pm if payload is None else jnp.sum(jnp.where(sel, payload, 0), axis=0, keepdims=True))
        s = jnp.where(sel, -jnp.inf, s)
    return vals, pays


def _peer_route_kernel(x_ref, nw_ref, wq_ref, k1_ref, k2_ref, h_ref, idx_ref, gate_ref):
    tm = x_ref.shape[0]
    k = PK_TOPK
    h = _rms_rows(x_ref[...], nw_ref[...])
    h_ref[...] = h
    q = _dot(h, wq_ref[...])
    key_iota = lax.broadcasted_iota(I32, (N_KEYS, tm), 0)
    r = lax.broadcasted_iota(I32, (80, 1), 0)
    prio = jnp.where(r < 16, r,
                     jnp.where(r < 72, (1 + ((r - 16) >> 3)) * k + ((r - 16) & 7), (r - 64) * k))
    idx_rows, gate_rows = [], []
    for hd in range(PK_HEADS):
        tops = []
        for half, kref in ((0, k1_ref), (1, k2_ref)):
            c0 = hd * 2 * N_KEYS + half * N_KEYS
            s_t = _dot_nt(kref[hd * N_KEYS:(hd + 1) * N_KEYS, :], q[:, c0:c0 + N_KEYS])
            vals, ids = _top_rows(s_t, key_iota, None, k)
            tops.append((jnp.concatenate(vals, axis=0), jnp.concatenate(ids, axis=0)))
        (v1, i1), (v2, i2) = tops
        cand_blocks = [v1[0:1] + v2]
        cidx_blocks = [i1[0:1] * N_KEYS + i2]
        for a in range(1, 8):
            cand_blocks.append(v1[a:a + 1] + v2[0:8])
            cidx_blocks.append(i1[a:a + 1] * N_KEYS + i2[0:8])
        cand_blocks.append(v1[8:16] + v2[0:1])
        cidx_blocks.append(i1[8:16] * N_KEYS + i2[0:1])
        cand = jnp.concatenate(cand_blocks, axis=0)
        cidx = jnp.concatenate(cidx_blocks, axis=0)
        sc, eidx = _top_rows(cand, prio, cidx, k)
        e = [jnp.exp(v - sc[0]) for v in sc]
        tot = functools.reduce(lambda a, b: a + b, e)
        gate_rows.extend([ek / tot for ek in e])
        idx_rows.extend(eidx)
    gate_ref[...] = jnp.concatenate(gate_rows, axis=0).T
    idx_ref[...] = jnp.concatenate(idx_rows, axis=0).T


def _peer_route(x2d, norm_w, wq_bf16, keys1_bf16, keys2_bf16, tm):
    t, d = x2d.shape
    const = lambda shape: pl.BlockSpec(shape, lambda i: (0, 0))
    return pl.pallas_call(
        _peer_route_kernel,
        grid=(t // tm,),
        in_specs=[pl.BlockSpec((tm, d), lambda i: (i, 0)), const((1, d)), const(wq_bf16.shape),
                  const(keys1_bf16.shape), const(keys2_bf16.shape)],
        out_specs=[pl.BlockSpec((tm, d), lambda i: (i, 0)),
                   pl.BlockSpec((tm, N_SEL), lambda i: (i, 0)),
                   pl.BlockSpec((tm, N_SEL), lambda i: (i, 0))],
        out_shape=[jax.ShapeDtypeStruct((t, d), F32),
                   jax.ShapeDtypeStruct((t, N_SEL), I32),
                   jax.ShapeDtypeStruct((t, N_SEL), F32)],
        compiler_params=_cparams("parallel"),
        name="peer_route",
    )(x2d, norm_w.reshape(1, d), wq_bf16, keys1_bf16, keys2_bf16)


def _peer_sc_body(n_tok, idx_hbm, gate_hbm, h_hbm, u_hbm, v_hbm, out_hbm,
                  idx_v, gate_v, x_v, out_v, u_buf, v_buf, tmp, cvec, sem_u, sem_v, sem_blk, sem_out):
    assert SC_AHEAD < SC_SLOTS and SC_AHEAD <= PK_HEADS
    d = h_hbm.shape[1]
    per_worker = n_tok // (SC_CORES * SC_SUBCORES)
    nblk = per_worker // SC_TOKB
    wid = lax.axis_index("s") * SC_CORES + lax.axis_index("c")
    nchunk = d // SC_LANES
    lanes = lax.iota(I32, SC_LANES)
    zero = jnp.zeros((SC_LANES,), F32)

    def blk_copies(b, bs):
        rows = pl.ds(wid * per_worker + b * SC_TOKB, SC_TOKB)
        return (pltpu.make_async_copy(idx_hbm.at[rows], idx_v.at[bs], sem_blk.at[bs]),
                pltpu.make_async_copy(gate_hbm.at[rows], gate_v.at[bs], sem_blk.at[bs]),
                pltpu.make_async_copy(h_hbm.at[rows], x_v.at[bs], sem_blk.at[bs]))

    def out_copy(b, bs):
        rows = pl.ds(wid * per_worker + b * SC_TOKB, SC_TOKB)
        return pltpu.make_async_copy(out_v.at[bs], out_hbm.at[rows], sem_out.at[bs])

    def copies(bs, t, hd, slot):
        ids = idx_v.at[bs, t, pl.ds(hd * PK_TOPK, PK_TOPK)]
        return (pltpu.make_async_copy(u_hbm.at[ids], u_buf.at[slot], sem_u.at[slot]),
                pltpu.make_async_copy(v_hbm.at[ids], v_buf.at[slot], sem_v.at[slot]))

    def start(bs, t, hd, slot):
        for cp in copies(bs, t, hd, slot):
            cp.start()

    def wait(bs, t, hd, slot):
        for cp in copies(bs, t, hd, slot):
            cp.wait()

    def compute(bs, t, hd, slot):
        def ubody(c, accs):
            sl = pl.ds(c * SC_LANES, SC_LANES)
            xv = x_v[bs, t, sl]
            return tuple(accs[e] + u_buf[slot, e, sl] * xv for e in range(PK_TOPK))

        accs = plsc.parallel_loop(0, nchunk, carry=(zero,) * PK_TOPK)(ubody)
        for e in range(PK_TOPK):
            tmp[e, :] = accs[e]
        a = zero
        for l in range(SC_LANES):
            a = a + plsc.load_gather(tmp, [lanes, (lanes + l) & (SC_LANES - 1)])
        z = 0.7978845608028654 * (a + 0.044715 * a * a * a)
        e2 = jnp.exp(-2.0 * jnp.abs(z))
        th = (1.0 - e2) / (1.0 + e2)
        th = jnp.where(z < 0.0, -th, th)
        cvec[pl.ds(SC_LANES, SC_LANES)] = gate_v[bs, t, pl.ds(hd * PK_TOPK, PK_TOPK)] * (0.5 * a * (1.0 + th))
        cb = [plsc.load_gather(cvec, [jnp.full((SC_LANES,), SC_LANES + e, I32)]) for e in range(PK_TOPK)]

        def vbody(c):
            sl = pl.ds(c * SC_LANES, SC_LANES)
            o = zero if hd == 0 else out_v[bs, t, sl]
            for e in range(PK_TOPK):
                o = o + cb[e] * v_buf[slot, e, sl]
            out_v[bs, t, sl] = o

        plsc.parallel_loop(0, nchunk)(vbody)

    for cp in blk_copies(0, 0):
        cp.start()
    for cp in blk_copies(0, 0):
        cp.wait()
    for n in range(SC_AHEAD):
        start(0, n // PK_HEADS, n % PK_HEADS, n % SC_SLOTS)

    def block(b, carry):
        bs = b % 2

        @pl.when(b + 1 < nblk)
        def _():
            for cp in blk_copies(b + 1, 1 - bs):
                cp.start()

        @pl.when(b >= 2)
        def _():
            out_copy(b - 2, bs).wait()

        def token(t, c2):
            g0 = (b * SC_TOKB + t) * PK_HEADS
            for hd in range(PK_HEADS):
                slot = (g0 + hd) % SC_SLOTS
                slot_ahead = (g0 + hd + SC_AHEAD) % SC_SLOTS
                hn = hd + SC_AHEAD
                if hn < PK_HEADS:
                    start(bs, t, hn, slot_ahead)
                else:
                    hn -= PK_HEADS

                    @pl.when(t + 1 < SC_TOKB)
                    def _():
                        start(bs, t + 1, hn, slot_ahead)

                    @pl.when(jnp.logical_and(t + 1 == SC_TOKB, b + 1 < nblk))
                    def _():
                        if hn == 0:
                            for cp in blk_copies(b + 1, 1 - bs):
                                cp.wait()
                        start(1 - bs, 0, hn, slot_ahead)
                wait(bs, t, hd, slot)
                compute(bs, t, hd, slot)
            return c2

        lax.fori_loop(0, SC_TOKB, token, 0)
        out_copy(b, bs).start()
        return carry

    lax.fori_loop(0, nblk, block, 0)
    for b in range(max(nblk - 2, 0), nblk):
        out_copy(b, b % 2).wait()


def _peer_sc(idx, gate, h, u_tab, v_tab, n_tok):
    d = h.shape[1]
    assert n_tok % (SC_CORES * SC_SUBCORES * SC_TOKB) == 0 and d % SC_LANES == 0
    mesh = plsc.VectorSubcoreMesh(core_axis_name="c", subcore_axis_name="s",
                                  num_cores=SC_CORES, num_subcores=SC_SUBCORES)
    return pl.kernel(
        functools.partial(_peer_sc_body, n_tok),
        out_type=jax.ShapeDtypeStruct((n_tok, d), F32),
        mesh=mesh,
        scratch_types=[
            pltpu.VMEM((2, SC_TOKB, N_SEL), I32),
            pltpu.VMEM((2, SC_TOKB, N_SEL), F32),
            pltpu.VMEM((2, SC_TOKB, d), F32),
            pltpu.VMEM((2, SC_TOKB, d), F32),
            pltpu.VMEM((SC_SLOTS, PK_TOPK, d), F32),
            pltpu.VMEM((SC_SLOTS, PK_TOPK, d), F32),
            pltpu.VMEM((PK_TOPK, SC_LANES), F32),
            pltpu.VMEM((2 * SC_LANES,), F32),
            pltpu.SemaphoreType.DMA((SC_SLOTS,)),
            pltpu.SemaphoreType.DMA((SC_SLOTS,)),
            pltpu.SemaphoreType.DMA((2,)),
            pltpu.SemaphoreType.DMA((2,)),
        ],
        compiler_params=pltpu.CompilerParams(needs_layout_passes=False),
        name="peer_sc",
    )(idx, gate, h, u_tab, v_tab)


def _final_kernel(x_ref, p_ref, nw_ref, out_ref):
    out_ref[...] = _rms_rows(x_ref[...] + p_ref[...], nw_ref[...])


def _final_norm(x2d, peer_out, norm_w, tm):
    t, d = peer_out.shape
    row = pl.BlockSpec((tm, d), lambda i: (i, 0))
    return pl.pallas_call(
        _final_kernel,
        grid=(t // tm,),
        in_specs=[row, row, pl.BlockSpec((1, d), lambda i: (0, 0))],
        out_specs=row,
        out_shape=jax.ShapeDtypeStruct((t, d), F32),
        compiler_params=_cparams("parallel"),
        name="final_norm",
    )(x2d, peer_out, norm_w.reshape(1, d))


def _trunk_piece(x, mem, p):
    bn, s, d = x.shape
    n_mem = mem.shape[1]
    x2d = x.reshape(bn * s, d)

    z = _rms_matmul(x2d, p["norm_mix_w"], p["w_in"], tm=256)
    hf, hb = _mlstm_scan(z, p["gate_b"], bn, s)
    x1 = _mixer_out(x2d, z, hf, hb, p["conv_w"], p["conv_b"], p["mlstm_norm_w"], p["w_out"], s, tm=256)

    kv = _rms_matmul(mem.reshape(bn * n_mem, d), p["norm_mem_w"], p["xa_wkv"], tm=256)
    x2 = _xattn(x1, kv, p["norm_xattn_w"], p["xa_wq"], p["xa_wo"], s, n_mem, tm=256)

    h3, eidx, gate = _peer_route(x2, p["norm_ffn_w"], p["peer_wq"], p["peer_keys1"], p["peer_keys2"], tm=256)
    peer_out = _peer_sc(eidx, gate, h3, p["peer_u"], p["peer_v"], bn * s)
    y = _final_norm(x2, peer_out, p["norm_final_w"], tm=512)
    return y.reshape(bn, s, d)


def _trunk(x, mem, p):
    bn, s, _ = x.shape
    rows = max(1, min(bn, PIECE_TOKENS // s))
    assert bn % rows == 0
    pieces = [_trunk_piece(x[i:i + rows], mem[i:i + rows], p) for i in range(0, bn, rows)]
    return pieces[0] if len(pieces) == 1 else jnp.concatenate(pieces, axis=0)


def kernel(x_prompt, x_sample, mem_prompt, mem_sample, norm_mix_w, w_in, gate_b, conv_w, conv_b, mlstm_norm_w, w_out, norm_xattn_w, norm_mem_w, xa_wq, xa_wk, xa_wv, xa_wo, norm_ffn_w, peer_wq, peer_keys1, peer_keys2, peer_u, peer_v, norm_final_w):
    assert w_in.shape[0] == 1, "single-layer trunk"
    d = x_prompt.shape[-1]
    gate_pad = LANES - GATE_COLS
    p = {
        "norm_mix_w": norm_mix_w[0],
        "w_in": jnp.pad(w_in[0], ((0, 0), (0, gate_pad))).astype(BF16),
        "gate_b": jnp.pad(gate_b[0], (0, gate_pad)).reshape(1, LANES),
        "conv_w": conv_w[0], "conv_b": conv_b[0], "mlstm_norm_w": mlstm_norm_w[0],
        "w_out": w_out[0].astype(BF16),
        "norm_xattn_w": norm_xattn_w[0], "norm_mem_w": norm_mem_w[0],
        "xa_wq": xa_wq[0].astype(BF16),
        "xa_wkv": jnp.concatenate([xa_wk[0], xa_wv[0]], axis=1).astype(BF16),
        "xa_wo": xa_wo[0].astype(BF16),
        "norm_ffn_w": norm_ffn_w[0],
        "peer_wq": peer_wq[0].astype(BF16),
        "peer_keys1": peer_keys1[0].reshape(PK_HEADS * N_KEYS, -1).astype(BF16),
        "peer_keys2": peer_keys2[0].reshape(PK_HEADS * N_KEYS, -1).astype(BF16),
        "peer_u": peer_u[0], "peer_v": peer_v[0],
        "norm_final_w": norm_final_w,
    }
    assert p["w_in"].shape[1] == Z_MAIN + LANES and d == 1024
    return (_trunk(x_prompt, mem_prompt, p), _trunk(x_sample, mem_sample, p))
```

```python
import functools

import jax
import jax.numpy as jnp
from jax import lax
from jax.experimental import pallas as pl
from jax.experimental.pallas import tpu as pltpu
from jax.experimental.pallas import tpu_sc as plsc

F32 = jnp.float32
BF16 = jnp.bfloat16
I32 = jnp.int32

EPS = 1e-6
M_INIT = -1e30
LANES = 128
SUBLANES = 8

CONV_W = 512
ML_HEADS = 4
ML_DH = 128
ML_W = ML_HEADS * ML_DH
CHUNK = 128
GATE_COLS = 4 * ML_HEADS
XA_HEADS = 4
N_KEYS = 128
PK_HEADS = 8
PK_TOPK = 16
N_SEL = PK_HEADS * PK_TOPK
VMEM_LIMIT = 56 * 1024 * 1024
SC_CORES = 2
SC_SUBCORES = 16
SC_LANES = 16
SC_TOKB = 4
SC_SLOTS = 3
SC_AHEAD = 2
PIECE_TOKENS = 4096

ZB_XT, ZB_BG, ZB_CG, ZB_Q, ZB_K, ZB_V, ZB_O = range(7)
Z_MAIN = 7 * 512
ZB_GATES = Z_MAIN // LANES


def _cparams(*sem):
    return pltpu.CompilerParams(dimension_semantics=sem, vmem_limit_bytes=VMEM_LIMIT)


def _rms_rows(x, w):
    return x * lax.rsqrt(jnp.mean(x * x, axis=-1, keepdims=True) + EPS) * w


def _log_sigmoid(x):
    return jnp.minimum(x, 0.0) - jnp.log(1.0 + jnp.exp(-jnp.abs(x)))


def _dot(a, b):
    return jnp.dot(a.astype(BF16), b.astype(BF16), preferred_element_type=F32)


def _dot_nt(a, b):
    return lax.dot_general(a.astype(BF16), b.astype(BF16), (((1,), (1,)), ((), ())),
                           preferred_element_type=F32)


def _dot_tn(a, b):
    return lax.dot_general(a.astype(BF16), b.astype(BF16), (((0,), (0,)), ((), ())),
                           preferred_element_type=F32)


def _rms_matmul_kernel(x_ref, nw_ref, w_ref, o_ref):
    h = _rms_rows(x_ref[...], nw_ref[...])
    o_ref[...] = _dot(h, w_ref[...])


def _rms_matmul(x2d, norm_w, w_bf16, tm):
    t, d = x2d.shape
    n = w_bf16.shape[1]
    return pl.pallas_call(
        _rms_matmul_kernel,
        grid=(t // tm,),
        in_specs=[pl.BlockSpec((tm, d), lambda i: (i, 0)),
                  pl.BlockSpec((1, d), lambda i: (0, 0)),
                  pl.BlockSpec((d, n), lambda i: (0, 0))],
        out_specs=pl.BlockSpec((tm, n), lambda i: (i, 0)),
        out_shape=jax.ShapeDtypeStruct((t, n), F32),
        compiler_params=_cparams("parallel"),
        name="rms_matmul",
    )(x2d, norm_w.reshape(1, d), w_bf16)


def _mlstm_direction(d, q_ref, k_ref, v_ref, g_ref, gb_ref, h_ref, c_ref, n_ref, m_ref):
    L = CHUNK
    r = lax.broadcasted_iota(I32, (L, L), 0)
    c = lax.broadcasted_iota(I32, (L, L), 1)
    allowed = (c <= r) if d == 0 else (c >= r)
    tri = allowed.astype(F32)
    g_all = g_ref[...] + gb_ref[...]
    lf = _log_sigmoid(g_all)
    a_all = jnp.dot(tri, lf, precision=lax.Precision.HIGHEST, preferred_element_type=F32)
    a_all_t = a_all.T
    g_all_t = g_all.T
    end = L - 1 if d == 0 else 0
    for hd in range(ML_HEADS):
        ci = d * ML_HEADS + hd
        cf = 2 * ML_HEADS + d * ML_HEADS + hd
        st = d * ML_HEADS + hd
        sl = slice(hd * ML_DH, (hd + 1) * ML_DH)
        q = q_ref[:, sl] * (ML_DH ** -0.5)
        k = k_ref[:, sl]
        v = v_ref[:, sl]
        a_col = a_all[:, cf:cf + 1]
        a_row = a_all_t[cf:cf + 1, :]
        ig_col = g_all[:, ci:ci + 1]
        ig_row = g_all_t[ci:ci + 1, :]
        g_tot = a_col[end:end + 1, :]
        c_prev = c_ref[st]
        n_prev = n_ref[st]
        m_prev = m_ref[st]

        dlog = jnp.where(allowed, a_col - a_row + ig_row, -jnp.inf)
        m_inter = a_col + m_prev
        m_t = jnp.maximum(m_inter, jnp.max(dlog, axis=1, keepdims=True))
        s = jnp.exp(dlog - m_t) * _dot_nt(q, k)
        e_inter = jnp.exp(m_inter - m_t)
        num = _dot(s, v) + e_inter * _dot(q, c_prev)
        den = jnp.sum(s, axis=1, keepdims=True) + e_inter * jnp.sum(q * n_prev, axis=1, keepdims=True)
        h_ref[:, sl] = num / jnp.maximum(jnp.abs(den), jnp.exp(-m_t))

        w_col = g_tot - a_col + ig_col
        m_new = jnp.maximum(g_tot + m_prev, jnp.max(w_col, axis=0, keepdims=True))
        decay = jnp.exp(g_tot + m_prev - m_new)
        ksc = k * jnp.exp(w_col - m_new)
        c_ref[st] = decay * c_prev + _dot_tn(ksc, v)
        n_ref[st] = decay * n_prev + jnp.sum(ksc, axis=0, keepdims=True)
        m_ref[st] = m_new


def _mlstm_scan_kernel(qf, kf, vf, gf, qb, kb, vb, gbk, gb_ref, hf_ref, hb_ref, c_ref, n_ref, m_ref):
    @pl.when(pl.program_id(1) == 0)
    def _():
        c_ref[...] = jnp.zeros_like(c_ref)
        n_ref[...] = jnp.zeros_like(n_ref)
        m_ref[...] = jnp.full_like(m_ref, M_INIT)

    _mlstm_direction(0, qf, kf, vf, gf, gb_ref, hf_ref, c_ref, n_ref, m_ref)
    _mlstm_direction(1, qb, kb, vb, gbk, gb_ref, hb_ref, c_ref, n_ref, m_ref)


def _mlstm_scan(z, gate_b_pad, bn, s):
    nc = s // CHUNK
    t = bn * s

    def fwd(col):
        return lambda b, i: (b * nc + i, col)

    def bwd(col):
        return lambda b, i: (b * nc + nc - 1 - i, col)

    wide = lambda im: pl.BlockSpec((CHUNK, 512), im)
    gate = lambda im: pl.BlockSpec((CHUNK, LANES), im)
    n_state = 2 * ML_HEADS
    return pl.pallas_call(
        _mlstm_scan_kernel,
        grid=(bn, nc),
        in_specs=[wide(fwd(ZB_Q)), wide(fwd(ZB_K)), wide(fwd(ZB_V)), gate(fwd(ZB_GATES)),
                  wide(bwd(ZB_Q)), wide(bwd(ZB_K)), wide(bwd(ZB_V)), gate(bwd(ZB_GATES)),
                  pl.BlockSpec((1, LANES), lambda b, i: (0, 0))],
        out_specs=[wide(fwd(0)), wide(bwd(0))],
        out_shape=[jax.ShapeDtypeStruct((t, ML_W), F32), jax.ShapeDtypeStruct((t, ML_W), F32)],
        scratch_shapes=[pltpu.VMEM((n_state, ML_DH, ML_DH), F32),
                        pltpu.VMEM((n_state, 1, ML_DH), F32),
                        pltpu.VMEM((n_state, 1, 1), F32)],
        compiler_params=_cparams("parallel", "arbitrary"),
        name="mlstm_scan",
    )(z, z, z, z, z, z, z, z, gate_b_pad)


def _mixer_out_kernel(tiles_per_seq, x_ref, xt_ref, bg_ref, cg_ref, o_ref, hf_ref, hb_ref,
                      xt_p, cg_p, xt_n, cg_n, cw_ref, cb_ref, mw_ref, wo_ref, out_ref):
    tm = x_ref.shape[0]
    pos = pl.program_id(0) % tiles_per_seq
    u = cg_ref[...] * xt_ref[...]
    has_prev = (pos > 0).astype(F32)
    has_next = (pos < tiles_per_seq - 1).astype(F32)
    u_before = (cg_p[...] * xt_p[...])[SUBLANES - 1:SUBLANES, :] * has_prev
    u_after = (cg_n[...] * xt_n[...])[0:1, :] * has_next
    row = lax.broadcasted_iota(I32, u.shape, 0)
    u_prev = jnp.where(row == 0, u_before, pltpu.roll(u, 1, axis=0))
    u_next = jnp.where(row == tm - 1, u_after, pltpu.roll(u, tm - 1, axis=0))
    conv = cw_ref[0:1, :] * u_prev + cw_ref[1:2, :] * u + cw_ref[2:3, :] * u_next + cb_ref[...]
    y_conv = bg_ref[...] * conv

    hm = hf_ref[...] + hb_ref[...]
    og = jax.nn.sigmoid(o_ref[...])
    acc = x_ref[...] + _dot(y_conv, wo_ref[0:CONV_W, :])
    for hd in range(ML_HEADS):
        sl = slice(hd * ML_DH, (hd + 1) * ML_DH)
        y_ml = og[:, sl] * _rms_rows(hm[:, sl], mw_ref[:, sl])
        acc += _dot(y_ml, wo_ref[CONV_W + hd * ML_DH:CONV_W + (hd + 1) * ML_DH, :])
    out_ref[...] = acc


def _mixer_out(x2d, z, hf, hb, conv_w, conv_b, ml_norm_w, w_out_bf16, s, tm):
    t, d = x2d.shape
    tiles_per_seq = s // tm
    rb = tm // SUBLANES
    last_rb = t // SUBLANES - 1
    wide = lambda col: pl.BlockSpec((tm, 512), lambda i: (i, col))
    halo_prev = lambda col: pl.BlockSpec((SUBLANES, 512), lambda i: (jnp.maximum(i * rb - 1, 0), col))
    halo_next = lambda col: pl.BlockSpec((SUBLANES, 512), lambda i: (jnp.minimum((i + 1) * rb, last_rb), col))
    const = lambda shape: pl.BlockSpec(shape, lambda i: (0, 0))
    return pl.pallas_call(
        functools.partial(_mixer_out_kernel, tiles_per_seq),
        grid=(t // tm,),
        in_specs=[pl.BlockSpec((tm, d), lambda i: (i, 0)),
                  wide(ZB_XT), wide(ZB_BG), wide(ZB_CG), wide(ZB_O),
                  wide(0), wide(0),
                  halo_prev(ZB_XT), halo_prev(ZB_CG), halo_next(ZB_XT), halo_next(ZB_CG),
                  const((3, CONV_W)), const((1, CONV_W)), const((1, ML_W)), const((d, d))],
        out_specs=pl.BlockSpec((tm, d), lambda i: (i, 0)),
        out_shape=jax.ShapeDtypeStruct((t, d), F32),
        compiler_params=_cparams("parallel"),
        name="mixer_out",
    )(x2d, z, z, z, z, hf, hb, z, z, z, z, conv_w, conv_b.reshape(1, -1), ml_norm_w.reshape(1, -1), w_out_bf16)


def _xattn_kernel(x_ref, kv_ref, nw_ref, wq_ref, wo_ref, out_ref):
    x = x_ref[...]
    d = x.shape[1]
    dh = d // XA_HEADS
    q = _dot(_rms_rows(x, nw_ref[...]), wq_ref[...])
    acc = x
    for hd in range(XA_HEADS):
        sl = slice(hd * dh, (hd + 1) * dh)
        s = _dot_nt(q[:, sl], kv_ref[:, sl]) * (dh ** -0.5)
        s = s - jnp.max(s, axis=-1, keepdims=True)
        p = jnp.exp(s)
        p = p / jnp.sum(p, axis=-1, keepdims=True)
        o = _dot(p, kv_ref[:, d + hd * dh:d + (hd + 1) * dh])
        acc += _dot(o, wo_ref[sl, :])
    out_ref[...] = acc


def _xattn(x2d, kv, norm_w, wq_bf16, wo_bf16, s, n_mem, tm):
    t, d = x2d.shape
    tiles_per_seq = s // tm
    const = lambda shape: pl.BlockSpec(shape, lambda i: (0, 0))
    return pl.pallas_call(
        _xattn_kernel,
        grid=(t // tm,),
        in_specs=[pl.BlockSpec((tm, d), lambda i: (i, 0)),
                  pl.BlockSpec((n_mem, 2 * d), lambda i: (i // tiles_per_seq, 0)),
                  const((1, d)), const((d, d)), const((d, d))],
        out_specs=pl.BlockSpec((tm, d), lambda i: (i, 0)),
        out_shape=jax.ShapeDtypeStruct((t, d), F32),
        compiler_params=_cparams("parallel"),
        name="xattn",
    )(x2d, kv, norm_w.reshape(1, d), wq_bf16, wo_bf16)


def _top_rows(s, prio, payload, k):
    big = jnp.int32(2 ** 30)
    vals, pays = [], []
    for _ in range(k):
        m = jnp.max(s, axis=0, keepdims=True)
        pm = jnp.min(jnp.where(s == m, prio, big), axis=0, keepdims=True)
        sel = prio == pm
        vals.append(m)
        pays.append(pm if payload is None else jnp.sum(jnp.where(sel, payload, 0), axis=0, keepdims=True))
        s = jnp.where(sel, -jnp.inf, s)
    return vals, pays


def _peer_route_kernel(x_ref, nw_ref, wq_ref, k1_ref, k2_ref, h_ref, idx_ref, gate_ref):
    tm = x_ref.shape[0]
    k = PK_TOPK
    h = _rms_rows(x_ref[...], nw_ref[...])
    h_ref[...] = h
    q = _dot(h, wq_ref[...])
    key_iota = lax.broadcasted_iota(I32, (N_KEYS, tm), 0)
    r = lax.broadcasted_iota(I32, (80, 1), 0)
    prio = jnp.where(r < 16, r,
                     jnp.where(r < 72, (1 + ((r - 16) >> 3)) * k + ((r - 16) & 7), (r - 64) * k))
    idx_rows, gate_rows = [], []
    for hd in range(PK_HEADS):
        tops = []
        for half, kref in ((0, k1_ref), (1, k2_ref)):
            c0 = hd * 2 * N_KEYS + half * N_KEYS
            s_t = _dot_nt(kref[hd * N_KEYS:(hd + 1) * N_KEYS, :], q[:, c0:c0 + N_KEYS])
            vals, ids = _top_rows(s_t, key_iota, None, k)
            tops.append((jnp.concatenate(vals, axis=0), jnp.concatenate(ids, axis=0)))
        (v1, i1), (v2, i2) = tops
        cand_blocks = [v1[0:1] + v2]
        cidx_blocks = [i1[0:1] * N_KEYS + i2]
        for a in range(1, 8):
            cand_blocks.append(v1[a:a + 1] + v2[0:8])
            cidx_blocks.append(i1[a:a + 1] * N_KEYS + i2[0:8])
        cand_blocks.append(v1[8:16] + v2[0:1])
        cidx_blocks.append(i1[8:16] * N_KEYS + i2[0:1])
        cand = jnp.concatenate(cand_blocks, axis=0)
        cidx = jnp.concatenate(cidx_blocks, axis=0)
        sc, eidx = _top_rows(cand, prio, cidx, k)
        e = [jnp.exp(v - sc[0]) for v in sc]
        tot = functools.reduce(lambda a, b: a + b, e)
        gate_rows.extend([ek / tot for ek in e])
        idx_rows.extend(eidx)
    gate_ref[...] = jnp.concatenate(gate_rows, axis=0).T
    idx_ref[...] = jnp.concatenate(idx_rows, axis=0).T


def _peer_route(x2d, norm_w, wq_bf16, keys1_bf16, keys2_bf16, tm):
    t, d = x2d.shape
    const = lambda shape: pl.BlockSpec(shape, lambda i: (0, 0))
    return pl.pallas_call(
        _peer_route_kernel,
        grid=(t // tm,),
        in_specs=[pl.BlockSpec((tm, d), lambda i: (i, 0)), const((1, d)), const(wq_bf16.shape),
                  const(keys1_bf16.shape), const(keys2_bf16.shape)],
        out_specs=[pl.BlockSpec((tm, d), lambda i: (i, 0)),
                   pl.BlockSpec((tm, N_SEL), lambda i: (i, 0)),
                   pl.BlockSpec((tm, N_SEL), lambda i: (i, 0))],
        out_shape=[jax.ShapeDtypeStruct((t, d), F32),
                   jax.ShapeDtypeStruct((t, N_SEL), I32),
                   jax.ShapeDtypeStruct((t, N_SEL), F32)],
        compiler_params=_cparams("parallel"),
        name="peer_route",
    )(x2d, norm_w.reshape(1, d), wq_bf16, keys1_bf16, keys2_bf16)


def _peer_sc_body(n_tok, idx_hbm, gate_hbm, h_hbm, u_hbm, v_hbm, out_hbm,
                  idx_v, gate_v, x_v, out_v, u_buf, v_buf, tmp, cvec, sem_u, sem_v, sem_blk, sem_out):
    assert SC_AHEAD < SC_SLOTS and SC_AHEAD <= PK_HEADS
    d = h_hbm.shape[1]
    per_worker = n_tok // (SC_CORES * SC_SUBCORES)
    nblk = per_worker // SC_TOKB
    wid = lax.axis_index("s") * SC_CORES + lax.axis_index("c")
    nchunk = d // SC_LANES
    lanes = lax.iota(I32, SC_LANES)
    zero = jnp.zeros((SC_LANES,), F32)

    def blk_copies(b, bs):
        rows = pl.ds(wid * per_worker + b * SC_TOKB, SC_TOKB)
        return (pltpu.make_async_copy(idx_hbm.at[rows], idx_v.at[bs], sem_blk.at[bs]),
                pltpu.make_async_copy(gate_hbm.at[rows], gate_v.at[bs], sem_blk.at[bs]),
                pltpu.make_async_copy(h_hbm.at[rows], x_v.at[bs], sem_blk.at[bs]))

    def out_copy(b, bs):
        rows = pl.ds(wid * per_worker + b * SC_TOKB, SC_TOKB)
        return pltpu.make_async_copy(out_v.at[bs], out_hbm.at[rows], sem_out.at[bs])

    def copies(bs, t, hd, slot):
        ids = idx_v.at[bs, t, pl.ds(hd * PK_TOPK, PK_TOPK)]
        return (pltpu.make_async_copy(u_hbm.at[ids], u_buf.at[slot], sem_u.at[slot]),
                pltpu.make_async_copy(v_hbm.at[ids], v_buf.at[slot], sem_v.at[slot]))

    def start(bs, t, hd, slot):
        for cp in copies(bs, t, hd, slot):
            cp.start()

    def wait(bs, t, hd, slot):
        for cp in copies(bs, t, hd, slot):
            cp.wait()

    def compute(bs, t, hd, slot):
        def ubody(c, accs):
            sl = pl.ds(c * SC_LANES, SC_LANES)
            xv = x_v[bs, t, sl]
            return tuple(accs[e] + u_buf[slot, e, sl] * xv for e in range(PK_TOPK))

        accs = plsc.parallel_loop(0, nchunk, carry=(zero,) * PK_TOPK)(ubody)
        for e in range(PK_TOPK):
            tmp[e, :] = accs[e]
        a = zero
        for l in range(SC_LANES):
            a = a + plsc.load_gather(tmp, [lanes, (lanes + l) & (SC_LANES - 1)])
        z = 0.7978845608028654 * (a + 0.044715 * a * a * a)
        e2 = jnp.exp(-2.0 * jnp.abs(z))
        th = (1.0 - e2) / (1.0 + e2)
        th = jnp.where(z < 0.0, -th, th)
        cvec[pl.ds(SC_LANES, SC_LANES)] = gate_v[bs, t, pl.ds(hd * PK_TOPK, PK_TOPK)] * (0.5 * a * (1.0 + th))
        cb = [plsc.load_gather(cvec, [jnp.full((SC_LANES,), SC_LANES + e, I32)]) for e in range(PK_TOPK)]

        def vbody(c):
            sl = pl.ds(c * SC_LANES, SC_LANES)
            o = zero if hd == 0 else out_v[bs, t, sl]
            for e in range(PK_TOPK):
                o = o + cb[e] * v_buf[slot, e, sl]
            out_v[bs, t, sl] = o

        plsc.parallel_loop(0, nchunk)(vbody)

    for cp in blk_copies(0, 0):
        cp.start()
    for cp in blk_copies(0, 0):
        cp.wait()
    for n in range(SC_AHEAD):
        start(0, n // PK_HEADS, n % PK_HEADS, n % SC_SLOTS)

    def block(b, carry):
        bs = b % 2

        @pl.when(b + 1 < nblk)
        def _():
            for cp in blk_copies(b + 1, 1 - bs):
                cp.start()

        @pl.when(b >= 2)
        def _():
            out_copy(b - 2, bs).wait()

        def token(t, c2):
            g0 = (b * SC_TOKB + t) * PK_HEADS
            for hd in range(PK_HEADS):
                slot = (g0 + hd) % SC_SLOTS
                slot_ahead = (g0 + hd + SC_AHEAD) % SC_SLOTS
                hn = hd + SC_AHEAD
                if hn < PK_HEADS:
                    start(bs, t, hn, slot_ahead)
                else:
                    hn -= PK_HEADS

                    @pl.when(t + 1 < SC_TOKB)
                    def _():
                        start(bs, t + 1, hn, slot_ahead)

                    @pl.when(jnp.logical_and(t + 1 == SC_TOKB, b + 1 < nblk))
                    def _():
                        if hn == 0:
                            for cp in blk_copies(b + 1, 1 - bs):
                                cp.wait()
                        start(1 - bs, 0, hn, slot_ahead)
                wait(bs, t, hd, slot)
                compute(bs, t, hd, slot)
            return c2

        lax.fori_loop(0, SC_TOKB, token, 0)
        out_copy(b, bs).start()
        return carry

    lax.fori_loop(0, nblk, block, 0)
    for b in range(max(nblk - 2, 0), nblk):
        out_copy(b, b % 2).wait()


def _peer_sc(idx, gate, h, u_tab, v_tab, n_tok):
    d = h.shape[1]
    assert n_tok % (SC_CORES * SC_SUBCORES * SC_TOKB) == 0 and d % SC_LANES == 0
    mesh = plsc.VectorSubcoreMesh(core_axis_name="c", subcore_axis_name="s",
                                  num_cores=SC_CORES, num_subcores=SC_SUBCORES)
    return pl.kernel(
        functools.partial(_peer_sc_body, n_tok),
        out_type=jax.ShapeDtypeStruct((n_tok, d), F32),
        mesh=mesh,
        scratch_types=[
            pltpu.VMEM((2, SC_TOKB, N_SEL), I32),
            pltpu.VMEM((2, SC_TOKB, N_SEL), F32),
            pltpu.VMEM((2, SC_TOKB, d), F32),
            pltpu.VMEM((2, SC_TOKB, d), F32),
            pltpu.VMEM((SC_SLOTS, PK_TOPK, d), F32),
            pltpu.VMEM((SC_SLOTS, PK_TOPK, d), F32),
            pltpu.VMEM((PK_TOPK, SC_LANES), F32),
            pltpu.VMEM((2 * SC_LANES,), F32),
            pltpu.SemaphoreType.DMA((SC_SLOTS,)),
            pltpu.SemaphoreType.DMA((SC_SLOTS,)),
            pltpu.SemaphoreType.DMA((2,)),
            pltpu.SemaphoreType.DMA((2,)),
        ],
        compiler_params=pltpu.CompilerParams(needs_layout_passes=False),
        name="peer_sc",
    )(idx, gate, h, u_tab, v_tab)


def _final_kernel(x_ref, p_ref, nw_ref, *rest):
    out_ref = rest[-1]
    out_ref[...] = _rms_rows(x_ref[...] + p_ref[...], nw_ref[...])


def _final_norm(x2d, peer_out, norm_w, y_full, row0, t_full, tm):
    t, d = peer_out.shape
    assert row0 % tm == 0 and t % tm == 0
    blk0 = row0 // tm
    row = pl.BlockSpec((tm, d), lambda i: (i, 0))
    in_specs = [row, row, pl.BlockSpec((1, d), lambda i: (0, 0))]
    operands = [x2d, peer_out, norm_w.reshape(1, d)]
    aliases = {}
    if y_full is not None:
        in_specs.append(pl.BlockSpec(memory_space=pl.ANY))
        operands.append(y_full)
        aliases = {3: 0}
    return pl.pallas_call(
        _final_kernel,
        grid=(t // tm,),
        in_specs=in_specs,
        out_specs=pl.BlockSpec((tm, d), lambda i: (i + blk0, 0)),
        out_shape=jax.ShapeDtypeStruct((t_full, d), F32),
        input_output_aliases=aliases,
        compiler_params=_cparams("parallel"),
        name="final_norm",
    )(*operands)


def _trunk_piece(x, mem, p, y_full, row0, t_full):
    bn, s, d = x.shape
    n_mem = mem.shape[1]
    x2d = x.reshape(bn * s, d)

    z = _rms_matmul(x2d, p["norm_mix_w"], p["w_in"], tm=256)
    hf, hb = _mlstm_scan(z, p["gate_b"], bn, s)
    x1 = _mixer_out(x2d, z, hf, hb, p["conv_w"], p["conv_b"], p["mlstm_norm_w"], p["w_out"], s, tm=256)

    kv = _rms_matmul(mem.reshape(bn * n_mem, d), p["norm_mem_w"], p["xa_wkv"], tm=256)
    x2 = _xattn(x1, kv, p["norm_xattn_w"], p["xa_wq"], p["xa_wo"], s, n_mem, tm=256)

    h3, eidx, gate = _peer_route(x2, p["norm_ffn_w"], p["peer_wq"], p["peer_keys1"], p["peer_keys2"], tm=256)
    peer_out = _peer_sc(eidx, gate, h3, p["peer_u"], p["peer_v"], bn * s)
    return _final_norm(x2, peer_out, p["norm_final_w"], y_full, row0, t_full, tm=512), eidx


def _trunk(x, mem, p, after=None):
    bn, s, d = x.shape
    rows = max(1, min(bn, PIECE_TOKENS // s))
    assert bn % rows == 0
    if after is not None:
        x, _ = lax.optimization_barrier((x, after))
    y = first = None
    for i in range(0, bn, rows):
        y, eidx = _trunk_piece(x[i:i + rows], mem[i:i + rows], p, y, i * s, bn * s)
        first = eidx if first is None else first
    return y.reshape(bn, s, d), first


def kernel(x_prompt, x_sample, mem_prompt, mem_sample, norm_mix_w, w_in, gate_b, conv_w, conv_b, mlstm_norm_w, w_out, norm_xattn_w, norm_mem_w, xa_wq, xa_wk, xa_wv, xa_wo, norm_ffn_w, peer_wq, peer_keys1, peer_keys2, peer_u, peer_v, norm_final_w):
    assert w_in.shape[0] == 1, "single-layer trunk"
    d = x_prompt.shape[-1]
    gate_pad = LANES - GATE_COLS
    p = {
        "norm_mix_w": norm_mix_w[0],
        "w_in": jnp.pad(w_in[0], ((0, 0), (0, gate_pad))).astype(BF16),
        "gate_b": jnp.pad(gate_b[0], (0, gate_pad)).reshape(1, LANES),
        "conv_w": conv_w[0], "conv_b": conv_b[0], "mlstm_norm_w": mlstm_norm_w[0],
        "w_out": w_out[0].astype(BF16),
        "norm_xattn_w": norm_xattn_w[0], "norm_mem_w": norm_mem_w[0],
        "xa_wq": xa_wq[0].astype(BF16),
        "xa_wkv": jnp.concatenate([xa_wk[0], xa_wv[0]], axis=1).astype(BF16),
        "xa_wo": xa_wo[0].astype(BF16),
        "norm_ffn_w": norm_ffn_w[0],
        "peer_wq": peer_wq[0].astype(BF16),
        "peer_keys1": peer_keys1[0].reshape(PK_HEADS * N_KEYS, -1).astype(BF16),
        "peer_keys2": peer_keys2[0].reshape(PK_HEADS * N_KEYS, -1).astype(BF16),
        "peer_u": peer_u[0], "peer_v": peer_v[0],
        "norm_final_w": norm_final_w,
    }
    assert p["w_in"].shape[1] == Z_MAIN + LANES and d == 1024
    groups = {"prompt": (x_prompt, mem_prompt), "sample": (x_sample, mem_sample)}
    order = sorted(groups, key=lambda g: groups[g][0].shape[1])
    out, after = {}, None
    for g in order:
        out[g], after = _trunk(*groups[g], p, after)
    return (out["prompt"], out["sample"])
```

```python
import functools

import jax
import jax.numpy as jnp
from jax import lax
from jax.experimental import pallas as pl
from jax.experimental.pallas import tpu as pltpu
from jax.experimental.pallas import tpu_sc as plsc

F32 = jnp.float32
BF16 = jnp.bfloat16
I32 = jnp.int32

EPS = 1e-6
M_INIT = -1e30
LANES = 128
SUBLANES = 8

CONV_W = 512
ML_HEADS = 4
ML_DH = 128
ML_W = ML_HEADS * ML_DH
CHUNK = 128
GATE_COLS = 4 * ML_HEADS
XA_HEADS = 4
N_KEYS = 128
PK_HEADS = 8
PK_TOPK = 16
N_SEL = PK_HEADS * PK_TOPK
VMEM_LIMIT = 56 * 1024 * 1024
SC_CORES = 2
SC_SUBCORES = 16
SC_LANES = 16
SC_TOKB = 4
SC_U_SLOTS = 4
SC_V_SLOTS = 4
SC_U_AHEAD = 3
SC_V_AHEAD = 2
PIECE_TOKENS = 4096

ZB_XT, ZB_BG, ZB_CG, ZB_Q, ZB_K, ZB_V, ZB_O = range(7)
Z_MAIN = 7 * 512
ZB_GATES = Z_MAIN // LANES


def _cparams(*sem):
    return pltpu.CompilerParams(dimension_semantics=sem, vmem_limit_bytes=VMEM_LIMIT)


def _rms_rows(x, w):
    return x * lax.rsqrt(jnp.mean(x * x, axis=-1, keepdims=True) + EPS) * w


def _log_sigmoid(x):
    return jnp.minimum(x, 0.0) - jnp.log(1.0 + jnp.exp(-jnp.abs(x)))


def _dot(a, b):
    return jnp.dot(a.astype(BF16), b.astype(BF16), preferred_element_type=F32)


def _dot_nt(a, b):
    return lax.dot_general(a.astype(BF16), b.astype(BF16), (((1,), (1,)), ((), ())),
                           preferred_element_type=F32)


def _dot_tn(a, b):
    return lax.dot_general(a.astype(BF16), b.astype(BF16), (((0,), (0,)), ((), ())),
                           preferred_element_type=F32)


def _rms_matmul_kernel(x_ref, nw_ref, w_ref, o_ref):
    h = _rms_rows(x_ref[...], nw_ref[...])
    o_ref[...] = _dot(h, w_ref[...])


def _rms_matmul(x2d, norm_w, w_bf16, tm):
    t, d = x2d.shape
    n = w_bf16.shape[1]
    return pl.pallas_call(
        _rms_matmul_kernel,
        grid=(t // tm,),
        in_specs=[pl.BlockSpec((tm, d), lambda i: (i, 0)),
                  pl.BlockSpec((1, d), lambda i: (0, 0)),
                  pl.BlockSpec((d, n), lambda i: (0, 0))],
        out_specs=pl.BlockSpec((tm, n), lambda i: (i, 0)),
        out_shape=jax.ShapeDtypeStruct((t, n), F32),
        compiler_params=_cparams("parallel"),
        name="rms_matmul",
    )(x2d, norm_w.reshape(1, d), w_bf16)


def _mlstm_direction(d, q_ref, k_ref, v_ref, g_ref, gb_ref, h_ref, c_ref, n_ref, m_ref):
    L = CHUNK
    r = lax.broadcasted_iota(I32, (L, L), 0)
    c = lax.broadcasted_iota(I32, (L, L), 1)
    allowed = (c <= r) if d == 0 else (c >= r)
    tri = allowed.astype(F32)
    g_all = g_ref[...] + gb_ref[...]
    lf = _log_sigmoid(g_all)
    a_all = jnp.dot(tri, lf, precision=lax.Precision.HIGHEST, preferred_element_type=F32)
    a_all_t = a_all.T
    g_all_t = g_all.T
    end = L - 1 if d == 0 else 0
    for hd in range(ML_HEADS):
        ci = d * ML_HEADS + hd
        cf = 2 * ML_HEADS + d * ML_HEADS + hd
        st = d * ML_HEADS + hd
        sl = slice(hd * ML_DH, (hd + 1) * ML_DH)
        q = q_ref[:, sl] * (ML_DH ** -0.5)
        k = k_ref[:, sl]
        v = v_ref[:, sl]
        a_col = a_all[:, cf:cf + 1]
        a_row = a_all_t[cf:cf + 1, :]
        ig_col = g_all[:, ci:ci + 1]
        ig_row = g_all_t[ci:ci + 1, :]
        g_tot = a_col[end:end + 1, :]
        c_prev = c_ref[st]
        n_prev = n_ref[st]
        m_prev = m_ref[st]

        dlog = jnp.where(allowed, a_col - a_row + ig_row, -jnp.inf)
        m_inter = a_col + m_prev
        m_t = jnp.maximum(m_inter, jnp.max(dlog, axis=1, keepdims=True))
        s = jnp.exp(dlog - m_t) * _dot_nt(q, k)
        e_inter = jnp.exp(m_inter - m_t)
        num = _dot(s, v) + e_inter * _dot(q, c_prev)
        den = jnp.sum(s, axis=1, keepdims=True) + e_inter * jnp.sum(q * n_prev, axis=1, keepdims=True)
        h_ref[:, sl] = num / jnp.maximum(jnp.abs(den), jnp.exp(-m_t))

        w_col = g_tot - a_col + ig_col
        m_new = jnp.maximum(g_tot + m_prev, jnp.max(w_col, axis=0, keepdims=True))
        decay = jnp.exp(g_tot + m_prev - m_new)
        ksc = k * jnp.exp(w_col - m_new)
        c_ref[st] = decay * c_prev + _dot_tn(ksc, v)
        n_ref[st] = decay * n_prev + jnp.sum(ksc, axis=0, keepdims=True)
        m_ref[st] = m_new


def _mlstm_scan_kernel(qf, kf, vf, gf, qb, kb, vb, gbk, gb_ref, hf_ref, hb_ref, c_ref, n_ref, m_ref):
    @pl.when(pl.program_id(1) == 0)
    def _():
        c_ref[...] = jnp.zeros_like(c_ref)
        n_ref[...] = jnp.zeros_like(n_ref)
        m_ref[...] = jnp.full_like(m_ref, M_INIT)

    _mlstm_direction(0, qf, kf, vf, gf, gb_ref, hf_ref, c_ref, n_ref, m_ref)
    _mlstm_direction(1, qb, kb, vb, gbk, gb_ref, hb_ref, c_ref, n_ref, m_ref)


def _mlstm_scan(z, gate_b_pad, bn, s):
    nc = s // CHUNK
    t = bn * s

    def fwd(col):
        return lambda b, i: (b * nc + i, col)

    def bwd(col):
        return lambda b, i: (b * nc + nc - 1 - i, col)

    wide = lambda im: pl.BlockSpec((CHUNK, 512), im)
    gate = lambda im: pl.BlockSpec((CHUNK, LANES), im)
    n_state = 2 * ML_HEADS
    return pl.pallas_call(
        _mlstm_scan_kernel,
        grid=(bn, nc),
        in_specs=[wide(fwd(ZB_Q)), wide(fwd(ZB_K)), wide(fwd(ZB_V)), gate(fwd(ZB_GATES)),
                  wide(bwd(ZB_Q)), wide(bwd(ZB_K)), wide(bwd(ZB_V)), gate(bwd(ZB_GATES)),
                  pl.BlockSpec((1, LANES), lambda b, i: (0, 0))],
        out_specs=[wide(fwd(0)), wide(bwd(0))],
        out_shape=[jax.ShapeDtypeStruct((t, ML_W), F32), jax.ShapeDtypeStruct((t, ML_W), F32)],
        scratch_shapes=[pltpu.VMEM((n_state, ML_DH, ML_DH), F32),
                        pltpu.VMEM((n_state, 1, ML_DH), F32),
                        pltpu.VMEM((n_state, 1, 1), F32)],
        compiler_params=_cparams("parallel", "arbitrary"),
        name="mlstm_scan",
    )(z, z, z, z, z, z, z, z, gate_b_pad)


def _mixer_out_kernel(tiles_per_seq, x_ref, xt_ref, bg_ref, cg_ref, o_ref, hf_ref, hb_ref,
                      xt_p, cg_p, xt_n, cg_n, cw_ref, cb_ref, mw_ref, wo_ref, out_ref):
    tm = x_ref.shape[0]
    pos = pl.program_id(0) % tiles_per_seq
    u = cg_ref[...] * xt_ref[...]
    has_prev = (pos > 0).astype(F32)
    has_next = (pos < tiles_per_seq - 1).astype(F32)
    u_before = (cg_p[...] * xt_p[...])[SUBLANES - 1:SUBLANES, :] * has_prev
    u_after = (cg_n[...] * xt_n[...])[0:1, :] * has_next
    row = lax.broadcasted_iota(I32, u.shape, 0)
    u_prev = jnp.where(row == 0, u_before, pltpu.roll(u, 1, axis=0))
    u_next = jnp.where(row == tm - 1, u_after, pltpu.roll(u, tm - 1, axis=0))
    conv = cw_ref[0:1, :] * u_prev + cw_ref[1:2, :] * u + cw_ref[2:3, :] * u_next + cb_ref[...]
    y_conv = bg_ref[...] * conv

    hm = hf_ref[...] + hb_ref[...]
    og = jax.nn.sigmoid(o_ref[...])
    acc = x_ref[...] + _dot(y_conv, wo_ref[0:CONV_W, :])
    for hd in range(ML_HEADS):
        sl = slice(hd * ML_DH, (hd + 1) * ML_DH)
        y_ml = og[:, sl] * _rms_rows(hm[:, sl], mw_ref[:, sl])
        acc += _dot(y_ml, wo_ref[CONV_W + hd * ML_DH:CONV_W + (hd + 1) * ML_DH, :])
    out_ref[...] = acc


def _mixer_out(x2d, z, hf, hb, conv_w, conv_b, ml_norm_w, w_out_bf16, s, tm):
    t, d = x2d.shape
    tiles_per_seq = s // tm
    rb = tm // SUBLANES
    last_rb = t // SUBLANES - 1
    wide = lambda col: pl.BlockSpec((tm, 512), lambda i: (i, col))
    halo_prev = lambda col: pl.BlockSpec((SUBLANES, 512), lambda i: (jnp.maximum(i * rb - 1, 0), col))
    halo_next = lambda col: pl.BlockSpec((SUBLANES, 512), lambda i: (jnp.minimum((i + 1) * rb, last_rb), col))
    const = lambda shape: pl.BlockSpec(shape, lambda i: (0, 0))
    return pl.pallas_call(
        functools.partial(_mixer_out_kernel, tiles_per_seq),
        grid=(t // tm,),
        in_specs=[pl.BlockSpec((tm, d), lambda i: (i, 0)),
                  wide(ZB_XT), wide(ZB_BG), wide(ZB_CG), wide(ZB_O),
                  wide(0), wide(0),
                  halo_prev(ZB_XT), halo_prev(ZB_CG), halo_next(ZB_XT), halo_next(ZB_CG),
                  const((3, CONV_W)), const((1, CONV_W)), const((1, ML_W)), const((d, d))],
        out_specs=pl.BlockSpec((tm, d), lambda i: (i, 0)),
        out_shape=jax.ShapeDtypeStruct((t, d), F32),
        compiler_params=_cparams("parallel"),
        name="mixer_out",
    )(x2d, z, z, z, z, hf, hb, z, z, z, z, conv_w, conv_b.reshape(1, -1), ml_norm_w.reshape(1, -1), w_out_bf16)


def _xattn_kernel(x_ref, kv_ref, nw_ref, wq_ref, wo_ref, out_ref):
    x = x_ref[...]
    d = x.shape[1]
    dh = d // XA_HEADS
    q = _dot(_rms_rows(x, nw_ref[...]), wq_ref[...])
    acc = x
    for hd in range(XA_HEADS):
        sl = slice(hd * dh, (hd + 1) * dh)
        s = _dot_nt(q[:, sl], kv_ref[:, sl]) * (dh ** -0.5)
        s = s - jnp.max(s, axis=-1, keepdims=True)
        p = jnp.exp(s)
        p = p / jnp.sum(p, axis=-1, keepdims=True)
        o = _dot(p, kv_ref[:, d + hd * dh:d + (hd + 1) * dh])
        acc += _dot(o, wo_ref[sl, :])
    out_ref[...] = acc


def _xattn(x2d, kv, norm_w, wq_bf16, wo_bf16, s, n_mem, tm):
    t, d = x2d.shape
    tiles_per_seq = s // tm
    const = lambda shape: pl.BlockSpec(shape, lambda i: (0, 0))
    return pl.pallas_call(
        _xattn_kernel,
        grid=(t // tm,),
        in_specs=[pl.BlockSpec((tm, d), lambda i: (i, 0)),
                  pl.BlockSpec((n_mem, 2 * d), lambda i: (i // tiles_per_seq, 0)),
                  const((1, d)), const((d, d)), const((d, d))],
        out_specs=pl.BlockSpec((tm, d), lambda i: (i, 0)),
        out_shape=jax.ShapeDtypeStruct((t, d), F32),
        compiler_params=_cparams("parallel"),
        name="xattn",
    )(x2d, kv, norm_w.reshape(1, d), wq_bf16, wo_bf16)


def _top_rows(s, prio, payload, k):
    big = jnp.int32(2 ** 30)
    vals, pays = [], []
    for _ in range(k):
        m = jnp.max(s, axis=0, keepdims=True)
        pm = jnp.min(jnp.where(s == m, prio, big), axis=0, keepdims=True)
        sel = prio == pm
        vals.append(m)
        pays.append(pm if payload is None else jnp.sum(jnp.where(sel, payload, 0), axis=0, keepdims=True))
        s = jnp.where(sel, -jnp.inf, s)
    return vals, pays


def _peer_route_kernel(x_ref, nw_ref, wq_ref, k1_ref, k2_ref, h_ref, idx_ref, gate_ref):
    tm = x_ref.shape[0]
    k = PK_TOPK
    h = _rms_rows(x_ref[...], nw_ref[...])
    h_ref[...] = h
    q = _dot(h, wq_ref[...])
    key_iota = lax.broadcasted_iota(I32, (N_KEYS, tm), 0)
    r = lax.broadcasted_iota(I32, (80, 1), 0)
    prio = jnp.where(r < 16, r,
                     jnp.where(r < 72, (1 + ((r - 16) >> 3)) * k + ((r - 16) & 7), (r - 64) * k))
    idx_rows, gate_rows = [], []
    for hd in range(PK_HEADS):
        tops = []
        for half, kref in ((0, k1_ref), (1, k2_ref)):
            c0 = hd * 2 * N_KEYS + half * N_KEYS
            s_t = _dot_nt(kref[hd * N_KEYS:(hd + 1) * N_KEYS, :], q[:, c0:c0 + N_KEYS])
            vals, ids = _top_rows(s_t, key_iota, None, k)
            tops.append((jnp.concatenate(vals, axis=0), jnp.concatenate(ids, axis=0)))
        (v1, i1), (v2, i2) = tops
        cand_blocks = [v1[0:1] + v2]
        cidx_blocks = [i1[0:1] * N_KEYS + i2]
        for a in range(1, 8):
            cand_blocks.append(v1[a:a + 1] + v2[0:8])
            cidx_blocks.append(i1[a:a + 1] * N_KEYS + i2[0:8])
        cand_blocks.append(v1[8:16] + v2[0:1])
        cidx_blocks.append(i1[8:16] * N_KEYS + i2[0:1])
        cand = jnp.concatenate(cand_blocks, axis=0)
        cidx = jnp.concatenate(cidx_blocks, axis=0)
        sc, eidx = _top_rows(cand, prio, cidx, k)
        e = [jnp.exp(v - sc[0]) for v in sc]
        tot = functools.reduce(lambda a, b: a + b, e)
        gate_rows.extend([ek / tot for ek in e])
        idx_rows.extend(eidx)
    gate_ref[...] = jnp.concatenate(gate_rows, axis=0).T
    idx_ref[...] = jnp.concatenate(idx_rows, axis=0).T


def _peer_route(x2d, norm_w, wq_bf16, keys1_bf16, keys2_bf16, tm):
    t, d = x2d.shape
    const = lambda shape: pl.BlockSpec(shape, lambda i: (0, 0))
    return pl.pallas_call(
        _peer_route_kernel,
        grid=(t // tm,),
        in_specs=[pl.BlockSpec((tm, d), lambda i: (i, 0)), const((1, d)), const(wq_bf16.shape),
                  const(keys1_bf16.shape), const(keys2_bf16.shape)],
        out_specs=[pl.BlockSpec((tm, d), lambda i: (i, 0)),
                   pl.BlockSpec((tm, N_SEL), lambda i: (i, 0)),
                   pl.BlockSpec((tm, N_SEL), lambda i: (i, 0))],
        out_shape=[jax.ShapeDtypeStruct((t, d), F32),
                   jax.ShapeDtypeStruct((t, N_SEL), I32),
                   jax.ShapeDtypeStruct((t, N_SEL), F32)],
        compiler_params=_cparams("parallel"),
        name="peer_route",
    )(x2d, norm_w.reshape(1, d), wq_bf16, keys1_bf16, keys2_bf16)


def _peer_sc_body(n_tok, idx_hbm, gate_hbm, h_hbm, u_hbm, v_hbm, out_hbm,
                  idx_v, gate_v, x_v, out_v, u_buf, v_buf, tmp, cvec, sem_u, sem_v, sem_blk, sem_out):
    assert SC_U_AHEAD < SC_U_SLOTS and SC_V_AHEAD + 1 < SC_V_SLOTS
    assert SC_V_AHEAD <= SC_U_AHEAD < PK_HEADS
    d = h_hbm.shape[1]
    per_worker = n_tok // (SC_CORES * SC_SUBCORES)
    nblk = per_worker // SC_TOKB
    wid = lax.axis_index("s") * SC_CORES + lax.axis_index("c")
    ngrp = d // (2 * SC_LANES)
    lanes = lax.iota(I32, SC_LANES)
    zero = jnp.zeros((SC_LANES,), F32)

    def blk_copies(b, bs):
        rows = pl.ds(wid * per_worker + b * SC_TOKB, SC_TOKB)
        return (pltpu.make_async_copy(idx_hbm.at[rows], idx_v.at[bs], sem_blk.at[bs]),
                pltpu.make_async_copy(gate_hbm.at[rows], gate_v.at[bs], sem_blk.at[bs]),
                pltpu.make_async_copy(h_hbm.at[rows], x_v.at[bs], sem_blk.at[bs]))

    def out_copy(b, bs):
        rows = pl.ds(wid * per_worker + b * SC_TOKB, SC_TOKB)
        return pltpu.make_async_copy(out_v.at[bs], out_hbm.at[rows], sem_out.at[bs])

    def u_copy(bs, t, hd, slot):
        ids = idx_v.at[bs, t, pl.ds(hd * PK_TOPK, PK_TOPK)]
        return pltpu.make_async_copy(u_hbm.at[ids], u_buf.at[slot], sem_u.at[slot])

    def v_copy(bs, t, hd, slot):
        ids = idx_v.at[bs, t, pl.ds(hd * PK_TOPK, PK_TOPK)]
        return pltpu.make_async_copy(v_hbm.at[ids], v_buf.at[slot], sem_v.at[slot])

    def store_coef(accs, bs, t, hd):
        for e in range(PK_TOPK):
            tmp[e, :] = accs[e]
        a = zero
        for l in range(SC_LANES):
            a = a + plsc.load_gather(tmp, [lanes, (lanes + l) & (SC_LANES - 1)])
        z = 0.7978845608028654 * (a + 0.044715 * a * a * a)
        e2 = jnp.exp(-2.0 * jnp.abs(z))
        th = (1.0 - e2) / (1.0 + e2)
        th = jnp.where(z < 0.0, -th, th)
        cvec[pl.ds(SC_LANES, SC_LANES)] = gate_v[bs, t, pl.ds(hd * PK_TOPK, PK_TOPK)] * (0.5 * a * (1.0 + th))

    def load_coef():
        return [plsc.load_gather(cvec, [jnp.full((SC_LANES,), SC_LANES + e, I32)]) for e in range(PK_TOPK)]

    def mix_v(cb, vslot, j, o0, o1):
        for e in range(PK_TOPK):
            w = v_buf[vslot, e, pl.ds(j * SC_LANES, SC_LANES)]
            o0 = o0 + cb[e] * plsc.bitcast(w << 16, F32)
            o1 = o1 + cb[e] * plsc.bitcast(w & jnp.int32(-65536), F32)
        return o0, o1

    def dot_u_mix_v(bs, t, hd, uslot, vslot, pbs, pt, v_init):
        cb = load_coef()

        def step(j, accs):
            sl0 = pl.ds(j * 2 * SC_LANES, SC_LANES)
            sl1 = pl.ds(j * 2 * SC_LANES + SC_LANES, SC_LANES)
            x0 = x_v[bs, t, sl0]
            x1 = x_v[bs, t, sl1]
            new = tuple(accs[e] + u_buf[uslot, e, sl0] * x0 + u_buf[uslot, e, sl1] * x1 for e in range(PK_TOPK))
            o0, o1 = (zero, zero) if v_init else (out_v[pbs, pt, sl0], out_v[pbs, pt, sl1])
            o0, o1 = mix_v(cb, vslot, j, o0, o1)
            out_v[pbs, pt, sl0] = o0
            out_v[pbs, pt, sl1] = o1
            return new

        accs = plsc.parallel_loop(0, ngrp, carry=(zero,) * PK_TOPK)(step)
        store_coef(accs, bs, t, hd)

    def mix_v_only(vslot, pbs, pt):
        cb = load_coef()

        def step(j):
            sl0 = pl.ds(j * 2 * SC_LANES, SC_LANES)
            sl1 = pl.ds(j * 2 * SC_LANES + SC_LANES, SC_LANES)
            o0, o1 = mix_v(cb, vslot, j, out_v[pbs, pt, sl0], out_v[pbs, pt, sl1])
            out_v[pbs, pt, sl0] = o0
            out_v[pbs, pt, sl1] = o1

        plsc.parallel_loop(0, ngrp)(step)

    cvec[pl.ds(SC_LANES, SC_LANES)] = zero
    for e in range(PK_TOPK):
        def zero_v(j, e=e):
            v_buf[SC_V_SLOTS - 1, e, pl.ds(j * SC_LANES, SC_LANES)] = jnp.zeros((SC_LANES,), I32)
        plsc.parallel_loop(0, ngrp)(zero_v)

    def zero_out(j):
        out_v[1, SC_TOKB - 1, pl.ds(j * SC_LANES, SC_LANES)] = zero
    plsc.parallel_loop(0, d // SC_LANES)(zero_out)

    for cp in blk_copies(0, 0):
        cp.start()
    for cp in blk_copies(0, 0):
        cp.wait()
    for n in range(SC_U_AHEAD):
        u_copy(0, n // PK_HEADS, n % PK_HEADS, n % SC_U_SLOTS).start()
    for n in range(SC_V_AHEAD):
        v_copy(0, n // PK_HEADS, n % PK_HEADS, n % SC_V_SLOTS).start()

    def block(b, carry):
        bs = b % 2

        @pl.when(b + 1 < nblk)
        def _():
            for cp in blk_copies(b + 1, 1 - bs):
                cp.start()

        @pl.when(b >= 2)
        def _():
            out_copy(b - 2, bs).wait()

        def token(t, c2):
            g0 = (b * SC_TOKB + t) * PK_HEADS
            for hd in range(PK_HEADS):
                i = g0 + hd
                for mk, ahead, nslots in ((u_copy, SC_U_AHEAD, SC_U_SLOTS), (v_copy, SC_V_AHEAD, SC_V_SLOTS)):
                    slot_ahead = (i + ahead) % nslots
                    hn = hd + ahead
                    if hn < PK_HEADS:
                        mk(bs, t, hn, slot_ahead).start()
                    else:
                        hn -= PK_HEADS

                        @pl.when(t + 1 < SC_TOKB)
                        def _(mk=mk, hn=hn, slot_ahead=slot_ahead):
                            mk(bs, t + 1, hn, slot_ahead).start()

                        @pl.when(jnp.logical_and(t + 1 == SC_TOKB, b + 1 < nblk))
                        def _(mk=mk, hn=hn, slot_ahead=slot_ahead):
                            if mk is u_copy and hn == 0:
                                for cp in blk_copies(b + 1, 1 - bs):
                                    cp.wait()
                            mk(1 - bs, 0, hn, slot_ahead).start()
                uslot = i % SC_U_SLOTS
                vslot = (i + SC_V_SLOTS - 1) % SC_V_SLOTS
                u_copy(bs, t, hd, uslot).wait()
                if hd >= 1:
                    v_copy(bs, t, hd - 1, vslot).wait()
                    dot_u_mix_v(bs, t, hd, uslot, vslot, bs, t, v_init=(hd == 1))
                else:
                    pbs = jnp.where(t == 0, 1 - bs, bs)
                    pt = jnp.where(t == 0, SC_TOKB - 1, t - 1)

                    @pl.when(i >= 1)
                    def _():
                        v_copy(pbs, pt, PK_HEADS - 1, vslot).wait()

                    dot_u_mix_v(bs, t, hd, uslot, vslot, pbs, pt, v_init=False)

                    @pl.when(jnp.logical_and(t == 0, b >= 1))
                    def _():
                        out_copy(b - 1, 1 - bs).start()
            return c2

        lax.fori_loop(0, SC_TOKB, token, 0)
        return carry

    lax.fori_loop(0, nblk, block, 0)
    last = nblk * SC_TOKB * PK_HEADS - 1
    lbs = (nblk - 1) % 2
    v_copy(lbs, SC_TOKB - 1, PK_HEADS - 1, last % SC_V_SLOTS).wait()
    mix_v_only(last % SC_V_SLOTS, lbs, SC_TOKB - 1)
    out_copy(nblk - 1, lbs).start()
    for b in range(max(nblk - 2, 0), nblk):
        out_copy(b, b % 2).wait()


def _pack_bf16_pairs(v):
    n, d = v.shape
    bits = lax.bitcast_convert_type(v.astype(BF16), jnp.uint16).astype(jnp.uint32)
    bits = bits.reshape(n, d // (2 * SC_LANES), 2, SC_LANES)
    words = (bits[:, :, 1, :] << 16) | bits[:, :, 0, :]
    return lax.bitcast_convert_type(words.reshape(n, d // 2), I32)


def _peer_sc(idx, gate, h, u_tab, v_packed, n_tok):
    d = h.shape[1]
    assert n_tok % (SC_CORES * SC_SUBCORES * SC_TOKB) == 0 and d % (2 * SC_LANES) == 0
    mesh = plsc.VectorSubcoreMesh(core_axis_name="c", subcore_axis_name="s",
                                  num_cores=SC_CORES, num_subcores=SC_SUBCORES)
    return pl.kernel(
        functools.partial(_peer_sc_body, n_tok),
        out_type=jax.ShapeDtypeStruct((n_tok, d), F32),
        mesh=mesh,
        scratch_types=[
            pltpu.VMEM((2, SC_TOKB, N_SEL), I32),
            pltpu.VMEM((2, SC_TOKB, N_SEL), F32),
            pltpu.VMEM((2, SC_TOKB, d), F32),
            pltpu.VMEM((2, SC_TOKB, d), F32),
            pltpu.VMEM((SC_U_SLOTS, PK_TOPK, d), F32),
            pltpu.VMEM((SC_V_SLOTS, PK_TOPK, d // 2), I32),
            pltpu.VMEM((PK_TOPK, SC_LANES), F32),
            pltpu.VMEM((2 * SC_LANES,), F32),
            pltpu.SemaphoreType.DMA((SC_U_SLOTS,)),
            pltpu.SemaphoreType.DMA((SC_V_SLOTS,)),
            pltpu.SemaphoreType.DMA((2,)),
            pltpu.SemaphoreType.DMA((2,)),
        ],
        compiler_params=pltpu.CompilerParams(needs_layout_passes=False),
        name="peer_sc",
    )(idx, gate, h, u_tab, v_packed)


def _final_kernel(x_ref, p_ref, nw_ref, *rest):
    out_ref = rest[-1]
    out_ref[...] = _rms_rows(x_ref[...] + p_ref[...], nw_ref[...])


def _final_norm(x2d, peer_out, norm_w, y_full, row0, t_full, tm):
    t, d = peer_out.shape
    assert row0 % tm == 0 and t % tm == 0
    blk0 = row0 // tm
    row = pl.BlockSpec((tm, d), lambda i: (i, 0))
    in_specs = [row, row, pl.BlockSpec((1, d), lambda i: (0, 0))]
    operands = [x2d, peer_out, norm_w.reshape(1, d)]
    aliases = {}
    if y_full is not None:
        in_specs.append(pl.BlockSpec(memory_space=pl.ANY))
        operands.append(y_full)
        aliases = {3: 0}
    return pl.pallas_call(
        _final_kernel,
        grid=(t // tm,),
        in_specs=in_specs,
        out_specs=pl.BlockSpec((tm, d), lambda i: (i + blk0, 0)),
        out_shape=jax.ShapeDtypeStruct((t_full, d), F32),
        input_output_aliases=aliases,
        compiler_params=_cparams("parallel"),
        name="final_norm",
    )(*operands)


def _trunk_piece(x, mem, p, y_full, row0, t_full):
    bn, s, d = x.shape
    n_mem = mem.shape[1]
    x2d = x.reshape(bn * s, d)

    z = _rms_matmul(x2d, p["norm_mix_w"], p["w_in"], tm=256)
    hf, hb = _mlstm_scan(z, p["gate_b"], bn, s)
    x1 = _mixer_out(x2d, z, hf, hb, p["conv_w"], p["conv_b"], p["mlstm_norm_w"], p["w_out"], s, tm=256)

    kv = _rms_matmul(mem.reshape(bn * n_mem, d), p["norm_mem_w"], p["xa_wkv"], tm=256)
    x2 = _xattn(x1, kv, p["norm_xattn_w"], p["xa_wq"], p["xa_wo"], s, n_mem, tm=256)

    h3, eidx, gate = _peer_route(x2, p["norm_ffn_w"], p["peer_wq"], p["peer_keys1"], p["peer_keys2"], tm=256)
    peer_out = _peer_sc(eidx, gate, h3, p["peer_u"], p["peer_v_packed"], bn * s)
    return _final_norm(x2, peer_out, p["norm_final_w"], y_full, row0, t_full, tm=512), eidx


def _trunk(x, mem, p, after=None):
    bn, s, d = x.shape
    rows = max(1, min(bn, PIECE_TOKENS // s))
    assert bn % rows == 0
    if after is not None:
        x, _ = lax.optimization_barrier((x, after))
    y = first = None
    for i in range(0, bn, rows):
        y, eidx = _trunk_piece(x[i:i + rows], mem[i:i + rows], p, y, i * s, bn * s)
        first = eidx if first is None else first
    return y.reshape(bn, s, d), first


def kernel(x_prompt, x_sample, mem_prompt, mem_sample, norm_mix_w, w_in, gate_b, conv_w, conv_b, mlstm_norm_w, w_out, norm_xattn_w, norm_mem_w, xa_wq, xa_wk, xa_wv, xa_wo, norm_ffn_w, peer_wq, peer_keys1, peer_keys2, peer_u, peer_v, norm_final_w):
    assert w_in.shape[0] == 1, "single-layer trunk"
    d = x_prompt.shape[-1]
    gate_pad = LANES - GATE_COLS
    p = {
        "norm_mix_w": norm_mix_w[0],
        "w_in": jnp.pad(w_in[0], ((0, 0), (0, gate_pad))).astype(BF16),
        "gate_b": jnp.pad(gate_b[0], (0, gate_pad)).reshape(1, LANES),
        "conv_w": conv_w[0], "conv_b": conv_b[0], "mlstm_norm_w": mlstm_norm_w[0],
        "w_out": w_out[0].astype(BF16),
        "norm_xattn_w": norm_xattn_w[0], "norm_mem_w": norm_mem_w[0],
        "xa_wq": xa_wq[0].astype(BF16),
        "xa_wkv": jnp.concatenate([xa_wk[0], xa_wv[0]], axis=1).astype(BF16),
        "xa_wo": xa_wo[0].astype(BF16),
        "norm_ffn_w": norm_ffn_w[0],
        "peer_wq": peer_wq[0].astype(BF16),
        "peer_keys1": peer_keys1[0].reshape(PK_HEADS * N_KEYS, -1).astype(BF16),
        "peer_keys2": peer_keys2[0].reshape(PK_HEADS * N_KEYS, -1).astype(BF16),
        "peer_u": peer_u[0], "peer_v_packed": _pack_bf16_pairs(peer_v[0]),
        "norm_final_w": norm_final_w,
    }
    assert p["w_in"].shape[1] == Z_MAIN + LANES and d == 1024
    groups = {"prompt": (x_prompt, mem_prompt), "sample": (x_sample, mem_sample)}
    order = sorted(groups, key=lambda g: groups[g][0].shape[1])
    out, after = {}, None
    for g in order:
        out[g], after = _trunk(*groups[g], p, after)
    return (out["prompt"], out["sample"])
```

```python
import functools

import jax
import jax.numpy as jnp
from jax import lax
from jax.experimental import pallas as pl
from jax.experimental.pallas import tpu as pltpu
from jax.experimental.pallas import tpu_sc as plsc

F32 = jnp.float32
BF16 = jnp.bfloat16
I32 = jnp.int32

EPS = 1e-6
M_INIT = -1e30
LANES = 128
SUBLANES = 8

CONV_W = 512
ML_HEADS = 4
ML_DH = 128
ML_W = ML_HEADS * ML_DH
CHUNK = 128
GATE_COLS = 4 * ML_HEADS
XA_HEADS = 4
N_KEYS = 128
PK_HEADS = 8
PK_TOPK = 16
N_SEL = PK_HEADS * PK_TOPK
VMEM_LIMIT = 56 * 1024 * 1024
SC_CORES = 2
SC_SUBCORES = 16
SC_LANES = 16
SC_TOKB = 4
SC_U_SLOTS = 4
SC_V_SLOTS = 4
SC_U_AHEAD = 3
SC_V_AHEAD = 2
PIECE_TOKENS = 4096
TC_PIECES = 4
TC_ROW_SLOTS = 4

ZB_XT, ZB_BG, ZB_CG, ZB_Q, ZB_K, ZB_V, ZB_O = range(7)
Z_MAIN = 7 * 512
ZB_GATES = Z_MAIN // LANES


def _cparams(*sem):
    return pltpu.CompilerParams(dimension_semantics=sem, vmem_limit_bytes=VMEM_LIMIT)


def _rms_rows(x, w):
    return x * lax.rsqrt(jnp.mean(x * x, axis=-1, keepdims=True) + EPS) * w


def _log_sigmoid(x):
    return jnp.minimum(x, 0.0) - jnp.log(1.0 + jnp.exp(-jnp.abs(x)))


def _dot(a, b):
    return jnp.dot(a.astype(BF16), b.astype(BF16), preferred_element_type=F32)


def _dot_nt(a, b):
    return lax.dot_general(a.astype(BF16), b.astype(BF16), (((1,), (1,)), ((), ())),
                           preferred_element_type=F32)


def _dot_tn(a, b):
    return lax.dot_general(a.astype(BF16), b.astype(BF16), (((0,), (0,)), ((), ())),
                           preferred_element_type=F32)


def _rms_matmul_kernel(x_ref, nw_ref, w_ref, o_ref):
    h = _rms_rows(x_ref[...], nw_ref[...])
    o_ref[...] = _dot(h, w_ref[...])


def _rms_matmul(x2d, norm_w, w_bf16, tm):
    t, d = x2d.shape
    n = w_bf16.shape[1]
    return pl.pallas_call(
        _rms_matmul_kernel,
        grid=(t // tm,),
        in_specs=[pl.BlockSpec((tm, d), lambda i: (i, 0)),
                  pl.BlockSpec((1, d), lambda i: (0, 0)),
                  pl.BlockSpec((d, n), lambda i: (0, 0))],
        out_specs=pl.BlockSpec((tm, n), lambda i: (i, 0)),
        out_shape=jax.ShapeDtypeStruct((t, n), F32),
        compiler_params=_cparams("parallel"),
        name="rms_matmul",
    )(x2d, norm_w.reshape(1, d), w_bf16)


def _mlstm_direction(d, q_ref, k_ref, v_ref, g_ref, gb_ref, h_ref, c_ref, n_ref, m_ref):
    L = CHUNK
    r = lax.broadcasted_iota(I32, (L, L), 0)
    c = lax.broadcasted_iota(I32, (L, L), 1)
    allowed = (c <= r) if d == 0 else (c >= r)
    tri = allowed.astype(F32)
    g_all = g_ref[...] + gb_ref[...]
    lf = _log_sigmoid(g_all)
    a_all = jnp.dot(tri, lf, precision=lax.Precision.HIGHEST, preferred_element_type=F32)
    a_all_t = a_all.T
    g_all_t = g_all.T
    end = L - 1 if d == 0 else 0
    for hd in range(ML_HEADS):
        ci = d * ML_HEADS + hd
        cf = 2 * ML_HEADS + d * ML_HEADS + hd
        st = d * ML_HEADS + hd
        sl = slice(hd * ML_DH, (hd + 1) * ML_DH)
        q = q_ref[:, sl] * (ML_DH ** -0.5)
        k = k_ref[:, sl]
        v = v_ref[:, sl]
        a_col = a_all[:, cf:cf + 1]
        a_row = a_all_t[cf:cf + 1, :]
        ig_col = g_all[:, ci:ci + 1]
        ig_row = g_all_t[ci:ci + 1, :]
        g_tot = a_col[end:end + 1, :]
        c_prev = c_ref[st]
        n_prev = n_ref[st]
        m_prev = m_ref[st]

        dlog = jnp.where(allowed, a_col - a_row + ig_row, -jnp.inf)
        m_inter = a_col + m_prev
        m_t = jnp.maximum(m_inter, jnp.max(dlog, axis=1, keepdims=True))
        s = jnp.exp(dlog - m_t) * _dot_nt(q, k)
        e_inter = jnp.exp(m_inter - m_t)
        num = _dot(s, v) + e_inter * _dot(q, c_prev)
        den = jnp.sum(s, axis=1, keepdims=True) + e_inter * jnp.sum(q * n_prev, axis=1, keepdims=True)
        h_ref[:, sl] = num / jnp.maximum(jnp.abs(den), jnp.exp(-m_t))

        w_col = g_tot - a_col + ig_col
        m_new = jnp.maximum(g_tot + m_prev, jnp.max(w_col, axis=0, keepdims=True))
        decay = jnp.exp(g_tot + m_prev - m_new)
        ksc = k * jnp.exp(w_col - m_new)
        c_ref[st] = decay * c_prev + _dot_tn(ksc, v)
        n_ref[st] = decay * n_prev + jnp.sum(ksc, axis=0, keepdims=True)
        m_ref[st] = m_new


def _mlstm_scan_kernel(qf, kf, vf, gf, qb, kb, vb, gbk, gb_ref, hf_ref, hb_ref, c_ref, n_ref, m_ref):
    @pl.when(pl.program_id(1) == 0)
    def _():
        c_ref[...] = jnp.zeros_like(c_ref)
        n_ref[...] = jnp.zeros_like(n_ref)
        m_ref[...] = jnp.full_like(m_ref, M_INIT)

    _mlstm_direction(0, qf, kf, vf, gf, gb_ref, hf_ref, c_ref, n_ref, m_ref)
    _mlstm_direction(1, qb, kb, vb, gbk, gb_ref, hb_ref, c_ref, n_ref, m_ref)


def _mlstm_scan(z, gate_b_pad, bn, s):
    nc = s // CHUNK
    t = bn * s

    def fwd(col):
        return lambda b, i: (b * nc + i, col)

    def bwd(col):
        return lambda b, i: (b * nc + nc - 1 - i, col)

    wide = lambda im: pl.BlockSpec((CHUNK, 512), im)
    gate = lambda im: pl.BlockSpec((CHUNK, LANES), im)
    n_state = 2 * ML_HEADS
    return pl.pallas_call(
        _mlstm_scan_kernel,
        grid=(bn, nc),
        in_specs=[wide(fwd(ZB_Q)), wide(fwd(ZB_K)), wide(fwd(ZB_V)), gate(fwd(ZB_GATES)),
                  wide(bwd(ZB_Q)), wide(bwd(ZB_K)), wide(bwd(ZB_V)), gate(bwd(ZB_GATES)),
                  pl.BlockSpec((1, LANES), lambda b, i: (0, 0))],
        out_specs=[wide(fwd(0)), wide(bwd(0))],
        out_shape=[jax.ShapeDtypeStruct((t, ML_W), F32), jax.ShapeDtypeStruct((t, ML_W), F32)],
        scratch_shapes=[pltpu.VMEM((n_state, ML_DH, ML_DH), F32),
                        pltpu.VMEM((n_state, 1, ML_DH), F32),
                        pltpu.VMEM((n_state, 1, 1), F32)],
        compiler_params=_cparams("parallel", "arbitrary"),
        name="mlstm_scan",
    )(z, z, z, z, z, z, z, z, gate_b_pad)


def _mixer_out_kernel(tiles_per_seq, x_ref, xt_ref, bg_ref, cg_ref, o_ref, hf_ref, hb_ref,
                      xt_p, cg_p, xt_n, cg_n, cw_ref, cb_ref, mw_ref, wo_ref, out_ref):
    tm = x_ref.shape[0]
    pos = pl.program_id(0) % tiles_per_seq
    u = cg_ref[...] * xt_ref[...]
    has_prev = (pos > 0).astype(F32)
    has_next = (pos < tiles_per_seq - 1).astype(F32)
    u_before = (cg_p[...] * xt_p[...])[SUBLANES - 1:SUBLANES, :] * has_prev
    u_after = (cg_n[...] * xt_n[...])[0:1, :] * has_next
    row = lax.broadcasted_iota(I32, u.shape, 0)
    u_prev = jnp.where(row == 0, u_before, pltpu.roll(u, 1, axis=0))
    u_next = jnp.where(row == tm - 1, u_after, pltpu.roll(u, tm - 1, axis=0))
    conv = cw_ref[0:1, :] * u_prev + cw_ref[1:2, :] * u + cw_ref[2:3, :] * u_next + cb_ref[...]
    y_conv = bg_ref[...] * conv

    hm = hf_ref[...] + hb_ref[...]
    og = jax.nn.sigmoid(o_ref[...])
    acc = x_ref[...] + _dot(y_conv, wo_ref[0:CONV_W, :])
    for hd in range(ML_HEADS):
        sl = slice(hd * ML_DH, (hd + 1) * ML_DH)
        y_ml = og[:, sl] * _rms_rows(hm[:, sl], mw_ref[:, sl])
        acc += _dot(y_ml, wo_ref[CONV_W + hd * ML_DH:CONV_W + (hd + 1) * ML_DH, :])
    out_ref[...] = acc


def _mixer_out(x2d, z, hf, hb, conv_w, conv_b, ml_norm_w, w_out_bf16, s, tm):
    t, d = x2d.shape
    tiles_per_seq = s // tm
    rb = tm // SUBLANES
    last_rb = t // SUBLANES - 1
    wide = lambda col: pl.BlockSpec((tm, 512), lambda i: (i, col))
    halo_prev = lambda col: pl.BlockSpec((SUBLANES, 512), lambda i: (jnp.maximum(i * rb - 1, 0), col))
    halo_next = lambda col: pl.BlockSpec((SUBLANES, 512), lambda i: (jnp.minimum((i + 1) * rb, last_rb), col))
    const = lambda shape: pl.BlockSpec(shape, lambda i: (0, 0))
    return pl.pallas_call(
        functools.partial(_mixer_out_kernel, tiles_per_seq),
        grid=(t // tm,),
        in_specs=[pl.BlockSpec((tm, d), lambda i: (i, 0)),
                  wide(ZB_XT), wide(ZB_BG), wide(ZB_CG), wide(ZB_O),
                  wide(0), wide(0),
                  halo_prev(ZB_XT), halo_prev(ZB_CG), halo_next(ZB_XT), halo_next(ZB_CG),
                  const((3, CONV_W)), const((1, CONV_W)), const((1, ML_W)), const((d, d))],
        out_specs=pl.BlockSpec((tm, d), lambda i: (i, 0)),
        out_shape=jax.ShapeDtypeStruct((t, d), F32),
        compiler_params=_cparams("parallel"),
        name="mixer_out",
    )(x2d, z, z, z, z, hf, hb, z, z, z, z, conv_w, conv_b.reshape(1, -1), ml_norm_w.reshape(1, -1), w_out_bf16)


def _xattn_kernel(x_ref, kv_ref, nw_ref, wq_ref, wo_ref, out_ref):
    x = x_ref[...]
    d = x.shape[1]
    dh = d // XA_HEADS
    q = _dot(_rms_rows(x, nw_ref[...]), wq_ref[...])
    acc = x
    for hd in range(XA_HEADS):
        sl = slice(hd * dh, (hd + 1) * dh)
        s = _dot_nt(q[:, sl], kv_ref[:, sl]) * (dh ** -0.5)
        s = s - jnp.max(s, axis=-1, keepdims=True)
        p = jnp.exp(s)
        p = p / jnp.sum(p, axis=-1, keepdims=True)
        o = _dot(p, kv_ref[:, d + hd * dh:d + (hd + 1) * dh])
        acc += _dot(o, wo_ref[sl, :])
    out_ref[...] = acc


def _xattn(x2d, kv, norm_w, wq_bf16, wo_bf16, s, n_mem, tm):
    t, d = x2d.shape
    tiles_per_seq = s // tm
    const = lambda shape: pl.BlockSpec(shape, lambda i: (0, 0))
    return pl.pallas_call(
        _xattn_kernel,
        grid=(t // tm,),
        in_specs=[pl.BlockSpec((tm, d), lambda i: (i, 0)),
                  pl.BlockSpec((n_mem, 2 * d), lambda i: (i // tiles_per_seq, 0)),
                  const((1, d)), const((d, d)), const((d, d))],
        out_specs=pl.BlockSpec((tm, d), lambda i: (i, 0)),
        out_shape=jax.ShapeDtypeStruct((t, d), F32),
        compiler_params=_cparams("parallel"),
        name="xattn",
    )(x2d, kv, norm_w.reshape(1, d), wq_bf16, wo_bf16)


def _top_rows(s, prio, payload, k):
    big = jnp.int32(2 ** 30)
    vals, pays = [], []
    for _ in range(k):
        m = jnp.max(s, axis=0, keepdims=True)
        pm = jnp.min(jnp.where(s == m, prio, big), axis=0, keepdims=True)
        sel = prio == pm
        vals.append(m)
        pays.append(pm if payload is None else jnp.sum(jnp.where(sel, payload, 0), axis=0, keepdims=True))
        s = jnp.where(sel, -jnp.inf, s)
    return vals, pays


def _peer_route_kernel(x_ref, nw_ref, wq_ref, k1_ref, k2_ref, h_ref, idx_ref, gate_ref):
    tm = x_ref.shape[0]
    k = PK_TOPK
    h = _rms_rows(x_ref[...], nw_ref[...])
    h_ref[...] = h
    q = _dot(h, wq_ref[...])
    key_iota = lax.broadcasted_iota(I32, (N_KEYS, tm), 0)
    r = lax.broadcasted_iota(I32, (80, 1), 0)
    prio = jnp.where(r < 16, r,
                     jnp.where(r < 72, (1 + ((r - 16) >> 3)) * k + ((r - 16) & 7), (r - 64) * k))
    idx_rows, gate_rows = [], []
    for hd in range(PK_HEADS):
        tops = []
        for half, kref in ((0, k1_ref), (1, k2_ref)):
            c0 = hd * 2 * N_KEYS + half * N_KEYS
            s_t = _dot_nt(kref[hd * N_KEYS:(hd + 1) * N_KEYS, :], q[:, c0:c0 + N_KEYS])
            vals, ids = _top_rows(s_t, key_iota, None, k)
            tops.append((jnp.concatenate(vals, axis=0), jnp.concatenate(ids, axis=0)))
        (v1, i1), (v2, i2) = tops
        cand_blocks = [v1[0:1] + v2]
        cidx_blocks = [i1[0:1] * N_KEYS + i2]
        for a in range(1, 8):
            cand_blocks.append(v1[a:a + 1] + v2[0:8])
            cidx_blocks.append(i1[a:a + 1] * N_KEYS + i2[0:8])
        cand_blocks.append(v1[8:16] + v2[0:1])
        cidx_blocks.append(i1[8:16] * N_KEYS + i2[0:1])
        cand = jnp.concatenate(cand_blocks, axis=0)
        cidx = jnp.concatenate(cidx_blocks, axis=0)
        sc, eidx = _top_rows(cand, prio, cidx, k)
        e = [jnp.exp(v - sc[0]) for v in sc]
        tot = functools.reduce(lambda a, b: a + b, e)
        gate_rows.extend([ek / tot for ek in e])
        idx_rows.extend(eidx)
    gate_ref[...] = jnp.concatenate(gate_rows, axis=0).T
    idx_ref[...] = jnp.concatenate(idx_rows, axis=0).T


def _peer_route(x2d, norm_w, wq_bf16, keys1_bf16, keys2_bf16, tm):
    t, d = x2d.shape
    const = lambda shape: pl.BlockSpec(shape, lambda i: (0, 0))
    return pl.pallas_call(
        _peer_route_kernel,
        grid=(t // tm,),
        in_specs=[pl.BlockSpec((tm, d), lambda i: (i, 0)), const((1, d)), const(wq_bf16.shape),
                  const(keys1_bf16.shape), const(keys2_bf16.shape)],
        out_specs=[pl.BlockSpec((tm, d), lambda i: (i, 0)),
                   pl.BlockSpec((tm, N_SEL), lambda i: (i, 0)),
                   pl.BlockSpec((tm, N_SEL), lambda i: (i, 0))],
        out_shape=[jax.ShapeDtypeStruct((t, d), F32),
                   jax.ShapeDtypeStruct((t, N_SEL), I32),
                   jax.ShapeDtypeStruct((t, N_SEL), F32)],
        compiler_params=_cparams("parallel"),
        name="peer_route",
    )(x2d, norm_w.reshape(1, d), wq_bf16, keys1_bf16, keys2_bf16)


def _peer_sc_body(n_tok, idx_hbm, gate_hbm, h_hbm, u_hbm, v_hbm, out_hbm,
                  idx_v, gate_v, x_v, out_v, u_buf, v_buf, tmp, cvec, sem_u, sem_v, sem_blk, sem_out):
    assert SC_U_AHEAD < SC_U_SLOTS and SC_V_AHEAD + 1 < SC_V_SLOTS
    assert SC_V_AHEAD <= SC_U_AHEAD < PK_HEADS
    d = h_hbm.shape[1]
    per_worker = n_tok // (SC_CORES * SC_SUBCORES)
    nblk = per_worker // SC_TOKB
    wid = lax.axis_index("s") * SC_CORES + lax.axis_index("c")
    ngrp = d // (2 * SC_LANES)
    lanes = lax.iota(I32, SC_LANES)
    zero = jnp.zeros((SC_LANES,), F32)

    def blk_copies(b, bs):
        rows = pl.ds(wid * per_worker + b * SC_TOKB, SC_TOKB)
        return (pltpu.make_async_copy(idx_hbm.at[rows], idx_v.at[bs], sem_blk.at[bs]),
                pltpu.make_async_copy(gate_hbm.at[rows], gate_v.at[bs], sem_blk.at[bs]),
                pltpu.make_async_copy(h_hbm.at[rows], x_v.at[bs], sem_blk.at[bs]))

    def out_copy(b, bs):
        rows = pl.ds(wid * per_worker + b * SC_TOKB, SC_TOKB)
        return pltpu.make_async_copy(out_v.at[bs], out_hbm.at[rows], sem_out.at[bs])

    def u_copy(bs, t, hd, slot):
        ids = idx_v.at[bs, t, pl.ds(hd * PK_TOPK, PK_TOPK)]
        return pltpu.make_async_copy(u_hbm.at[ids], u_buf.at[slot], sem_u.at[slot])

    def v_copy(bs, t, hd, slot):
        ids = idx_v.at[bs, t, pl.ds(hd * PK_TOPK, PK_TOPK)]
        return pltpu.make_async_copy(v_hbm.at[ids], v_buf.at[slot], sem_v.at[slot])

    def store_coef(accs, bs, t, hd):
        for e in range(PK_TOPK):
            tmp[e, :] = accs[e]
        a = zero
        for l in range(SC_LANES):
            a = a + plsc.load_gather(tmp, [lanes, (lanes + l) & (SC_LANES - 1)])
        z = 0.7978845608028654 * (a + 0.044715 * a * a * a)
        e2 = jnp.exp(-2.0 * jnp.abs(z))
        th = (1.0 - e2) / (1.0 + e2)
        th = jnp.where(z < 0.0, -th, th)
        cvec[pl.ds(SC_LANES, SC_LANES)] = gate_v[bs, t, pl.ds(hd * PK_TOPK, PK_TOPK)] * (0.5 * a * (1.0 + th))

    def load_coef():
        return [plsc.load_gather(cvec, [jnp.full((SC_LANES,), SC_LANES + e, I32)]) for e in range(PK_TOPK)]

    def mix_v(cb, vslot, j, o0, o1):
        for e in range(PK_TOPK):
            w = v_buf[vslot, e, pl.ds(j * SC_LANES, SC_LANES)]
            o0 = o0 + cb[e] * plsc.bitcast(w << 16, F32)
            o1 = o1 + cb[e] * plsc.bitcast(w & jnp.int32(-65536), F32)
        return o0, o1

    def dot_u_mix_v(bs, t, hd, uslot, vslot, pbs, pt, v_init):
        cb = load_coef()

        def step(j, accs):
            sl0 = pl.ds(j * 2 * SC_LANES, SC_LANES)
            sl1 = pl.ds(j * 2 * SC_LANES + SC_LANES, SC_LANES)
            x0 = x_v[bs, t, sl0]
            x1 = x_v[bs, t, sl1]
            new = tuple(accs[e] + u_buf[uslot, e, sl0] * x0 + u_buf[uslot, e, sl1] * x1 for e in range(PK_TOPK))
            o0, o1 = (zero, zero) if v_init else (out_v[pbs, pt, sl0], out_v[pbs, pt, sl1])
            o0, o1 = mix_v(cb, vslot, j, o0, o1)
            out_v[pbs, pt, sl0] = o0
            out_v[pbs, pt, sl1] = o1
            return new

        accs = plsc.parallel_loop(0, ngrp, carry=(zero,) * PK_TOPK)(step)
        store_coef(accs, bs, t, hd)

    def mix_v_only(vslot, pbs, pt):
        cb = load_coef()

        def step(j):
            sl0 = pl.ds(j * 2 * SC_LANES, SC_LANES)
            sl1 = pl.ds(j * 2 * SC_LANES + SC_LANES, SC_LANES)
            o0, o1 = mix_v(cb, vslot, j, out_v[pbs, pt, sl0], out_v[pbs, pt, sl1])
            out_v[pbs, pt, sl0] = o0
            out_v[pbs, pt, sl1] = o1

        plsc.parallel_loop(0, ngrp)(step)

    cvec[pl.ds(SC_LANES, SC_LANES)] = zero
    for e in range(PK_TOPK):
        def zero_v(j, e=e):
            v_buf[SC_V_SLOTS - 1, e, pl.ds(j * SC_LANES, SC_LANES)] = jnp.zeros((SC_LANES,), I32)
        plsc.parallel_loop(0, ngrp)(zero_v)

    def zero_out(j):
        out_v[1, SC_TOKB - 1, pl.ds(j * SC_LANES, SC_LANES)] = zero
    plsc.parallel_loop(0, d // SC_LANES)(zero_out)

    for cp in blk_copies(0, 0):
        cp.start()
    for cp in blk_copies(0, 0):
        cp.wait()
    for n in range(SC_U_AHEAD):
        u_copy(0, n // PK_HEADS, n % PK_HEADS, n % SC_U_SLOTS).start()
    for n in range(SC_V_AHEAD):
        v_copy(0, n // PK_HEADS, n % PK_HEADS, n % SC_V_SLOTS).start()

    def block(b, carry):
        bs = b % 2

        @pl.when(b + 1 < nblk)
        def _():
            for cp in blk_copies(b + 1, 1 - bs):
                cp.start()

        @pl.when(b >= 2)
        def _():
            out_copy(b - 2, bs).wait()

        def token(t, c2):
            g0 = (b * SC_TOKB + t) * PK_HEADS
            for hd in range(PK_HEADS):
                i = g0 + hd
                for mk, ahead, nslots in ((u_copy, SC_U_AHEAD, SC_U_SLOTS), (v_copy, SC_V_AHEAD, SC_V_SLOTS)):
                    slot_ahead = (i + ahead) % nslots
                    hn = hd + ahead
                    if hn < PK_HEADS:
                        mk(bs, t, hn, slot_ahead).start()
                    else:
                        hn -= PK_HEADS

                        @pl.when(t + 1 < SC_TOKB)
                        def _(mk=mk, hn=hn, slot_ahead=slot_ahead):
                            mk(bs, t + 1, hn, slot_ahead).start()

                        @pl.when(jnp.logical_and(t + 1 == SC_TOKB, b + 1 < nblk))
                        def _(mk=mk, hn=hn, slot_ahead=slot_ahead):
                            if mk is u_copy and hn == 0:
                                for cp in blk_copies(b + 1, 1 - bs):
                                    cp.wait()
                            mk(1 - bs, 0, hn, slot_ahead).start()
                uslot = i % SC_U_SLOTS
                vslot = (i + SC_V_SLOTS - 1) % SC_V_SLOTS
                u_copy(bs, t, hd, uslot).wait()
                if hd >= 1:
                    v_copy(bs, t, hd - 1, vslot).wait()
                    dot_u_mix_v(bs, t, hd, uslot, vslot, bs, t, v_init=(hd == 1))
                else:
                    pbs = jnp.where(t == 0, 1 - bs, bs)
                    pt = jnp.where(t == 0, SC_TOKB - 1, t - 1)

                    @pl.when(i >= 1)
                    def _():
                        v_copy(pbs, pt, PK_HEADS - 1, vslot).wait()

                    dot_u_mix_v(bs, t, hd, uslot, vslot, pbs, pt, v_init=False)

                    @pl.when(jnp.logical_and(t == 0, b >= 1))
                    def _():
                        out_copy(b - 1, 1 - bs).start()
            return c2

        lax.fori_loop(0, SC_TOKB, token, 0)
        return carry

    lax.fori_loop(0, nblk, block, 0)
    last = nblk * SC_TOKB * PK_HEADS - 1
    lbs = (nblk - 1) % 2
    v_copy(lbs, SC_TOKB - 1, PK_HEADS - 1, last % SC_V_SLOTS).wait()
    mix_v_only(last % SC_V_SLOTS, lbs, SC_TOKB - 1)
    out_copy(nblk - 1, lbs).start()
    for b in range(max(nblk - 2, 0), nblk):
        out_copy(b, b % 2).wait()


def _pack_bf16_pairs(v):
    n, d = v.shape
    bits = lax.bitcast_convert_type(v.astype(BF16), jnp.uint16).astype(jnp.uint32)
    bits = bits.reshape(n, d // (2 * SC_LANES), 2, SC_LANES)
    words = (bits[:, :, 1, :] << 16) | bits[:, :, 0, :]
    return lax.bitcast_convert_type(words.reshape(n, d // 2), I32)


def _peer_sc(idx, gate, h, u_tab, v_packed, n_tok):
    d = h.shape[1]
    assert n_tok % (SC_CORES * SC_SUBCORES * SC_TOKB) == 0 and d % (2 * SC_LANES) == 0
    mesh = plsc.VectorSubcoreMesh(core_axis_name="c", subcore_axis_name="s",
                                  num_cores=SC_CORES, num_subcores=SC_SUBCORES)
    return pl.kernel(
        functools.partial(_peer_sc_body, n_tok),
        out_type=jax.ShapeDtypeStruct((n_tok, d), F32),
        mesh=mesh,
        scratch_types=[
            pltpu.VMEM((2, SC_TOKB, N_SEL), I32),
            pltpu.VMEM((2, SC_TOKB, N_SEL), F32),
            pltpu.VMEM((2, SC_TOKB, d), F32),
            pltpu.VMEM((2, SC_TOKB, d), F32),
            pltpu.VMEM((SC_U_SLOTS, PK_TOPK, d), F32),
            pltpu.VMEM((SC_V_SLOTS, PK_TOPK, d // 2), I32),
            pltpu.VMEM((PK_TOPK, SC_LANES), F32),
            pltpu.VMEM((2 * SC_LANES,), F32),
            pltpu.SemaphoreType.DMA((SC_U_SLOTS,)),
            pltpu.SemaphoreType.DMA((SC_V_SLOTS,)),
            pltpu.SemaphoreType.DMA((2,)),
            pltpu.SemaphoreType.DMA((2,)),
        ],
        compiler_params=pltpu.CompilerParams(needs_layout_passes=False),
        name="peer_sc",
    )(idx, gate, h, u_tab, v_packed)


def _final_kernel(x_ref, p_ref, nw_ref, *rest):
    out_ref = rest[-1]
    out_ref[...] = _rms_rows(x_ref[...] + p_ref[...], nw_ref[...])


def _final_norm(x2d, peer_out, norm_w, y_full, row0, t_full, tm):
    t, d = peer_out.shape
    assert row0 % tm == 0 and t % tm == 0
    blk0 = row0 // tm
    row = pl.BlockSpec((tm, d), lambda i: (i, 0))
    in_specs = [row, row, pl.BlockSpec((1, d), lambda i: (0, 0))]
    operands = [x2d, peer_out, norm_w.reshape(1, d)]
    aliases = {}
    if y_full is not None:
        in_specs.append(pl.BlockSpec(memory_space=pl.ANY))
        operands.append(y_full)
        aliases = {3: 0}
    return pl.pallas_call(
        _final_kernel,
        grid=(t // tm,),
        in_specs=in_specs,
        out_specs=pl.BlockSpec((tm, d), lambda i: (i + blk0, 0)),
        out_shape=jax.ShapeDtypeStruct((t_full, d), F32),
        input_output_aliases=aliases,
        compiler_params=_cparams("parallel"),
        name="final_norm",
    )(*operands)


def _gelu_tanh(x):
    return 0.5 * x * (1.0 + jnp.tanh(0.7978845608028654 * (x + 0.044715 * x * x * x)))


def _peer_tc_kernel(idx_hbm, uv_hbm, h_ref, gate_ref, x_ref, nw_ref, y_any, out_ref,
                    idx_smem, rows, acc_ref, gate_t_ref, idx_sem, row_sem):
    del y_any
    tb, d = h_ref.shape
    blk = pl.program_id(0)

    idx_copy = pltpu.make_async_copy(idx_hbm.at[pl.ds(blk * (tb * N_SEL), tb * N_SEL)], idx_smem, idx_sem)
    idx_copy.start()
    idx_copy.wait()

    def start_rows(tok, slot):
        for e in range(N_SEL):
            pltpu.make_async_copy(uv_hbm.at[pl.ds(idx_smem[tok * N_SEL + e], 1)],
                                  rows.at[slot, pl.ds(e, 1)], row_sem.at[slot]).start()

    def wait_rows(slot):
        pltpu.make_async_copy(uv_hbm.at[pl.ds(0, N_SEL)], rows.at[slot], row_sem.at[slot]).wait()

    lane = lax.broadcasted_iota(I32, (N_SEL, tb), 1)
    gate_t_ref[...] = gate_ref[...].T

    for j in range(TC_ROW_SLOTS - 1):
        start_rows(j, j)

    def body(j, carry):
        slot = j % TC_ROW_SLOTS
        ahead = j + TC_ROW_SLOTS - 1

        @pl.when(ahead < tb)
        def _():
            start_rows(ahead, ahead % TC_ROW_SLOTS)

        wait_rows(slot)
        x_row = h_ref[pl.ds(j, 1), :]
        u = rows[slot, :, 0:d]
        a = jnp.sum(u * x_row, axis=1, keepdims=True)
        g = jnp.sum(jnp.where(lane == j, gate_t_ref[...], 0.0), axis=1, keepdims=True)
        coef = g * _gelu_tanh(a)
        acc_ref[pl.ds(j, 1), :] = jnp.sum(coef * rows[slot, :, d:2 * d], axis=0, keepdims=True)
        return carry

    lax.fori_loop(0, tb, body, 0)
    out_ref[...] = _rms_rows(x_ref[...] + acc_ref[...], nw_ref[...])


def _peer_tc(idx_flat, uv, h, gate, x2d, norm_w, y_full, row0, t_full, tb):
    t, d = x2d.shape
    assert tb == N_SEL and t % tb == 0 and row0 % tb == 0 and y_full.shape == (t_full, d)
    blk0 = row0 // tb
    row = pl.BlockSpec((tb, d), lambda i: (i, 0))
    return pl.pallas_call(
        _peer_tc_kernel,
        grid=(t // tb,),
        in_specs=[pl.BlockSpec(memory_space=pl.ANY), pl.BlockSpec(memory_space=pl.ANY),
                  row, pl.BlockSpec((tb, N_SEL), lambda i: (i, 0)), row,
                  pl.BlockSpec((1, d), lambda i: (0, 0)), pl.BlockSpec(memory_space=pl.ANY)],
        out_specs=pl.BlockSpec((tb, d), lambda i: (i + blk0, 0)),
        out_shape=jax.ShapeDtypeStruct((t_full, d), F32),
        input_output_aliases={6: 0},
        scratch_shapes=[pltpu.SMEM((tb * N_SEL,), I32),
                        pltpu.VMEM((TC_ROW_SLOTS, N_SEL, 2 * d), F32),
                        pltpu.VMEM((tb, d), F32),
                        pltpu.VMEM((N_SEL, tb), F32),
                        pltpu.SemaphoreType.DMA(()),
                        pltpu.SemaphoreType.DMA((TC_ROW_SLOTS,))],
        compiler_params=_cparams("arbitrary"),
        name="peer_tc",
    )(idx_flat, uv, h, gate, x2d, norm_w.reshape(1, d), y_full)


def _dense_stages(x, mem, p):
    bn, s, d = x.shape
    n_mem = mem.shape[1]
    x2d = x.reshape(bn * s, d)

    z = _rms_matmul(x2d, p["norm_mix_w"], p["w_in"], tm=256)
    hf, hb = _mlstm_scan(z, p["gate_b"], bn, s)
    x1 = _mixer_out(x2d, z, hf, hb, p["conv_w"], p["conv_b"], p["mlstm_norm_w"], p["w_out"], s, tm=256)

    kv = _rms_matmul(mem.reshape(bn * n_mem, d), p["norm_mem_w"], p["xa_wkv"], tm=256)
    x2 = _xattn(x1, kv, p["norm_xattn_w"], p["xa_wq"], p["xa_wo"], s, n_mem, tm=256)

    h3, eidx, gate = _peer_route(x2, p["norm_ffn_w"], p["peer_wq"], p["peer_keys1"], p["peer_keys2"], tm=256)
    return x2, h3, eidx, gate


def _layer(groups, p):
    assert 0 < TC_PIECES < groups[0][0].shape[0] * groups[0][0].shape[1] // PIECE_TOKENS
    sc_pieces = [[] for _ in groups]
    handle = last_route = None
    tc_pieces = []
    for gi, (x, mem) in enumerate(groups):
        bn, s, d = x.shape
        rows = max(1, min(bn, PIECE_TOKENS // s))
        assert bn % rows == 0
        if handle is not None:
            x, _ = lax.optimization_barrier((x, handle))
        for i in range(0, bn, rows):
            x2, h3, eidx, gate = _dense_stages(x[i:i + rows], mem[i:i + rows], p)
            handle = eidx if i == 0 else handle
            last_route = eidx
            if gi == 0 and i + rows * TC_PIECES >= bn:
                tc_pieces.append((eidx, gate, h3, x2, i * s))
            else:
                peer_out = _peer_sc(eidx, gate, h3, p["peer_u"], p["peer_v_packed"], rows * s)
                sc_pieces[gi].append((x2, peer_out, i * s))
    outs = []
    for gi, (x, _) in enumerate(groups):
        t_full = x.shape[0] * x.shape[1]
        y = None
        for x2, peer_out, row0 in sc_pieces[gi]:
            y = _final_norm(x2, peer_out, p["norm_final_w"], y, row0, t_full, tm=512)
        for eidx, gate, h3, x2, row0 in (tc_pieces if gi == 0 else []):
            eidx, _ = lax.optimization_barrier((eidx, last_route))
            y = _peer_tc(eidx.reshape(-1), p["peer_uv"], h3, gate, x2, p["norm_final_w"], y, row0, t_full, tb=N_SEL)
        outs.append(y.reshape(x.shape))
    return outs


def kernel(x_prompt, x_sample, mem_prompt, mem_sample, norm_mix_w, w_in, gate_b, conv_w, conv_b, mlstm_norm_w, w_out, norm_xattn_w, norm_mem_w, xa_wq, xa_wk, xa_wv, xa_wo, norm_ffn_w, peer_wq, peer_keys1, peer_keys2, peer_u, peer_v, norm_final_w):
    assert w_in.shape[0] == 1, "single-layer trunk"
    d = x_prompt.shape[-1]
    gate_pad = LANES - GATE_COLS
    p = {
        "norm_mix_w": norm_mix_w[0],
        "w_in": jnp.pad(w_in[0], ((0, 0), (0, gate_pad))).astype(BF16),
        "gate_b": jnp.pad(gate_b[0], (0, gate_pad)).reshape(1, LANES),
        "conv_w": conv_w[0], "conv_b": conv_b[0], "mlstm_norm_w": mlstm_norm_w[0],
        "w_out": w_out[0].astype(BF16),
        "norm_xattn_w": norm_xattn_w[0], "norm_mem_w": norm_mem_w[0],
        "xa_wq": xa_wq[0].astype(BF16),
        "xa_wkv": jnp.concatenate([xa_wk[0], xa_wv[0]], axis=1).astype(BF16),
        "xa_wo": xa_wo[0].astype(BF16),
        "norm_ffn_w": norm_ffn_w[0],
        "peer_wq": peer_wq[0].astype(BF16),
        "peer_keys1": peer_keys1[0].reshape(PK_HEADS * N_KEYS, -1).astype(BF16),
        "peer_keys2": peer_keys2[0].reshape(PK_HEADS * N_KEYS, -1).astype(BF16),
        "peer_u": peer_u[0], "peer_v_packed": _pack_bf16_pairs(peer_v[0]),
        "peer_uv": jnp.concatenate([peer_u[0], peer_v[0].astype(BF16).astype(F32)], axis=1),
        "norm_final_w": norm_final_w,
    }
    assert p["w_in"].shape[1] == Z_MAIN + LANES and d == 1024
    groups = {"prompt": (x_prompt, mem_prompt), "sample": (x_sample, mem_sample)}
    order = sorted(groups, key=lambda g: groups[g][0].shape[1])
    out = dict(zip(order, _layer([groups[g] for g in order], p)))
    return (out["prompt"], out["sample"])
```

```python
import functools

import jax
import jax.numpy as jnp
from jax import lax
from jax.experimental import pallas as pl
from jax.experimental.pallas import tpu as pltpu
from jax.experimental.pallas import tpu_sc as plsc

F32 = jnp.float32
BF16 = jnp.bfloat16
I32 = jnp.int32

EPS = 1e-6
M_INIT = -1e30
LANES = 128
SUBLANES = 8

CONV_W = 512
ML_HEADS = 4
ML_DH = 128
ML_W = ML_HEADS * ML_DH
CHUNK = 128
GATE_COLS = 4 * ML_HEADS
XA_HEADS = 4
N_KEYS = 128
PK_HEADS = 8
PK_TOPK = 16
N_SEL = PK_HEADS * PK_TOPK
VMEM_LIMIT = 56 * 1024 * 1024
SC_CORES = 2
SC_SUBCORES = 16
SC_LANES = 16
SC_TOKB = 4
SC_U_SLOTS = 4
SC_V_SLOTS = 4
SC_U_AHEAD = 3
SC_V_AHEAD = 2
PIECE_TOKENS = 4096
TC_PIECES = 4
TC_ROW_SLOTS = 4

ZB_XT, ZB_BG, ZB_CG, ZB_Q, ZB_K, ZB_V, ZB_O = range(7)
Z_MAIN = 7 * 512
ZB_GATES = Z_MAIN // LANES


def _cparams(*sem):
    return pltpu.CompilerParams(dimension_semantics=sem, vmem_limit_bytes=VMEM_LIMIT)


def _rms_rows(x, w):
    return x * lax.rsqrt(jnp.mean(x * x, axis=-1, keepdims=True) + EPS) * w


def _log_sigmoid(x):
    return jnp.minimum(x, 0.0) - jnp.log(1.0 + jnp.exp(-jnp.abs(x)))


def _dot(a, b):
    return jnp.dot(a.astype(BF16), b.astype(BF16), preferred_element_type=F32)


def _dot_nt(a, b):
    return lax.dot_general(a.astype(BF16), b.astype(BF16), (((1,), (1,)), ((), ())),
                           preferred_element_type=F32)


def _dot_tn(a, b):
    return lax.dot_general(a.astype(BF16), b.astype(BF16), (((0,), (0,)), ((), ())),
                           preferred_element_type=F32)


def _rms_matmul_kernel(x_ref, nw_ref, w_ref, o_ref):
    h = _rms_rows(x_ref[...], nw_ref[...])
    o_ref[...] = _dot(h, w_ref[...])


def _rms_matmul(x2d, norm_w, w_bf16, tm):
    t, d = x2d.shape
    n = w_bf16.shape[1]
    return pl.pallas_call(
        _rms_matmul_kernel,
        grid=(t // tm,),
        in_specs=[pl.BlockSpec((tm, d), lambda i: (i, 0)),
                  pl.BlockSpec((1, d), lambda i: (0, 0)),
                  pl.BlockSpec((d, n), lambda i: (0, 0))],
        out_specs=pl.BlockSpec((tm, n), lambda i: (i, 0)),
        out_shape=jax.ShapeDtypeStruct((t, n), F32),
        compiler_params=_cparams("parallel"),
        name="rms_matmul",
    )(x2d, norm_w.reshape(1, d), w_bf16)


def _mlstm_direction(d, q_ref, k_ref, v_ref, g_ref, gb_ref, h_ref, c_ref, n_ref, m_ref):
    L = CHUNK
    r = lax.broadcasted_iota(I32, (L, L), 0)
    c = lax.broadcasted_iota(I32, (L, L), 1)
    allowed = (c <= r) if d == 0 else (c >= r)
    tri = allowed.astype(F32)
    g_all = g_ref[...] + gb_ref[...]
    lf = _log_sigmoid(g_all)
    a_all = jnp.dot(tri, lf, precision=lax.Precision.HIGHEST, preferred_element_type=F32)
    a_all_t = a_all.T
    g_all_t = g_all.T
    end = L - 1 if d == 0 else 0
    for hd in range(ML_HEADS):
        ci = d * ML_HEADS + hd
        cf = 2 * ML_HEADS + d * ML_HEADS + hd
        st = d * ML_HEADS + hd
        sl = slice(hd * ML_DH, (hd + 1) * ML_DH)
        q = q_ref[:, sl] * (ML_DH ** -0.5)
        k = k_ref[:, sl]
        v = v_ref[:, sl]
        a_col = a_all[:, cf:cf + 1]
        a_row = a_all_t[cf:cf + 1, :]
        ig_col = g_all[:, ci:ci + 1]
        ig_row = g_all_t[ci:ci + 1, :]
        g_tot = a_col[end:end + 1, :]
        c_prev = c_ref[st]
        n_prev = n_ref[st]
        m_prev = m_ref[st]

        dlog = jnp.where(allowed, a_col - a_row + ig_row, -jnp.inf)
        m_inter = a_col + m_prev
        m_t = jnp.maximum(m_inter, jnp.max(dlog, axis=1, keepdims=True))
        s = jnp.exp(dlog - m_t) * _dot_nt(q, k)
        e_inter = jnp.exp(m_inter - m_t)
        num = _dot(s, v) + e_inter * _dot(q, c_prev)
        den = jnp.sum(s, axis=1, keepdims=True) + e_inter * jnp.sum(q * n_prev, axis=1, keepdims=True)
        h_ref[:, sl] = num / jnp.maximum(jnp.abs(den), jnp.exp(-m_t))

        w_col = g_tot - a_col + ig_col
        m_new = jnp.maximum(g_tot + m_prev, jnp.max(w_col, axis=0, keepdims=True))
        decay = jnp.exp(g_tot + m_prev - m_new)
        ksc = k * jnp.exp(w_col - m_new)
        c_ref[st] = decay * c_prev + _dot_tn(ksc, v)
        n_ref[st] = decay * n_prev + jnp.sum(ksc, axis=0, keepdims=True)
        m_ref[st] = m_new


def _mlstm_scan_kernel(qf, kf, vf, gf, qb, kb, vb, gbk, gb_ref, hf_ref, hb_ref, c_ref, n_ref, m_ref):
    @pl.when(pl.program_id(1) == 0)
    def _():
        c_ref[...] = jnp.zeros_like(c_ref)
        n_ref[...] = jnp.zeros_like(n_ref)
        m_ref[...] = jnp.full_like(m_ref, M_INIT)

    _mlstm_direction(0, qf, kf, vf, gf, gb_ref, hf_ref, c_ref, n_ref, m_ref)
    _mlstm_direction(1, qb, kb, vb, gbk, gb_ref, hb_ref, c_ref, n_ref, m_ref)


def _mlstm_scan(z, gate_b_pad, bn, s):
    nc = s // CHUNK
    t = bn * s

    def fwd(col):
        return lambda b, i: (b * nc + i, col)

    def bwd(col):
        return lambda b, i: (b * nc + nc - 1 - i, col)

    wide = lambda im: pl.BlockSpec((CHUNK, 512), im)
    gate = lambda im: pl.BlockSpec((CHUNK, LANES), im)
    n_state = 2 * ML_HEADS
    return pl.pallas_call(
        _mlstm_scan_kernel,
        grid=(bn, nc),
        in_specs=[wide(fwd(ZB_Q)), wide(fwd(ZB_K)), wide(fwd(ZB_V)), gate(fwd(ZB_GATES)),
                  wide(bwd(ZB_Q)), wide(bwd(ZB_K)), wide(bwd(ZB_V)), gate(bwd(ZB_GATES)),
                  pl.BlockSpec((1, LANES), lambda b, i: (0, 0))],
        out_specs=[wide(fwd(0)), wide(bwd(0))],
        out_shape=[jax.ShapeDtypeStruct((t, ML_W), F32), jax.ShapeDtypeStruct((t, ML_W), F32)],
        scratch_shapes=[pltpu.VMEM((n_state, ML_DH, ML_DH), F32),
                        pltpu.VMEM((n_state, 1, ML_DH), F32),
                        pltpu.VMEM((n_state, 1, 1), F32)],
        compiler_params=_cparams("parallel", "arbitrary"),
        name="mlstm_scan",
    )(z, z, z, z, z, z, z, z, gate_b_pad)


def _mixer_out_kernel(tiles_per_seq, x_ref, xt_ref, bg_ref, cg_ref, o_ref, hf_ref, hb_ref,
                      xt_p, cg_p, xt_n, cg_n, cw_ref, cb_ref, mw_ref, wo_ref, out_ref):
    tm = x_ref.shape[0]
    pos = pl.program_id(0) % tiles_per_seq
    u = cg_ref[...] * xt_ref[...]
    has_prev = (pos > 0).astype(F32)
    has_next = (pos < tiles_per_seq - 1).astype(F32)
    u_before = (cg_p[...] * xt_p[...])[SUBLANES - 1:SUBLANES, :] * has_prev
    u_after = (cg_n[...] * xt_n[...])[0:1, :] * has_next
    row = lax.broadcasted_iota(I32, u.shape, 0)
    u_prev = jnp.where(row == 0, u_before, pltpu.roll(u, 1, axis=0))
    u_next = jnp.where(row == tm - 1, u_after, pltpu.roll(u, tm - 1, axis=0))
    conv = cw_ref[0:1, :] * u_prev + cw_ref[1:2, :] * u + cw_ref[2:3, :] * u_next + cb_ref[...]
    y_conv = bg_ref[...] * conv

    hm = hf_ref[...] + hb_ref[...]
    og = jax.nn.sigmoid(o_ref[...])
    acc = x_ref[...] + _dot(y_conv, wo_ref[0:CONV_W, :])
    for hd in range(ML_HEADS):
        sl = slice(hd * ML_DH, (hd + 1) * ML_DH)
        y_ml = og[:, sl] * _rms_rows(hm[:, sl], mw_ref[:, sl])
        acc += _dot(y_ml, wo_ref[CONV_W + hd * ML_DH:CONV_W + (hd + 1) * ML_DH, :])
    out_ref[...] = acc


def _mixer_out(x2d, z, hf, hb, conv_w, conv_b, ml_norm_w, w_out_bf16, s, tm):
    t, d = x2d.shape
    tiles_per_seq = s // tm
    rb = tm // SUBLANES
    last_rb = t // SUBLANES - 1
    wide = lambda col: pl.BlockSpec((tm, 512), lambda i: (i, col))
    halo_prev = lambda col: pl.BlockSpec((SUBLANES, 512), lambda i: (jnp.maximum(i * rb - 1, 0), col))
    halo_next = lambda col: pl.BlockSpec((SUBLANES, 512), lambda i: (jnp.minimum((i + 1) * rb, last_rb), col))
    const = lambda shape: pl.BlockSpec(shape, lambda i: (0, 0))
    return pl.pallas_call(
        functools.partial(_mixer_out_kernel, tiles_per_seq),
        grid=(t // tm,),
        in_specs=[pl.BlockSpec((tm, d), lambda i: (i, 0)),
                  wide(ZB_XT), wide(ZB_BG), wide(ZB_CG), wide(ZB_O),
                  wide(0), wide(0),
                  halo_prev(ZB_XT), halo_prev(ZB_CG), halo_next(ZB_XT), halo_next(ZB_CG),
                  const((3, CONV_W)), const((1, CONV_W)), const((1, ML_W)), const((d, d))],
        out_specs=pl.BlockSpec((tm, d), lambda i: (i, 0)),
        out_shape=jax.ShapeDtypeStruct((t, d), F32),
        compiler_params=_cparams("parallel"),
        name="mixer_out",
    )(x2d, z, z, z, z, hf, hb, z, z, z, z, conv_w, conv_b.reshape(1, -1), ml_norm_w.reshape(1, -1), w_out_bf16)


def _xattn_kernel(x_ref, kv_ref, nw_ref, wq_ref, wo_ref, out_ref):
    x = x_ref[...]
    d = x.shape[1]
    dh = d // XA_HEADS
    q = _dot(_rms_rows(x, nw_ref[...]), wq_ref[...])
    acc = x
    for hd in range(XA_HEADS):
        sl = slice(hd * dh, (hd + 1) * dh)
        s = _dot_nt(q[:, sl], kv_ref[:, sl]) * (dh ** -0.5)
        s = s - jnp.max(s, axis=-1, keepdims=True)
        p = jnp.exp(s)
        p = p / jnp.sum(p, axis=-1, keepdims=True)
        o = _dot(p, kv_ref[:, d + hd * dh:d + (hd + 1) * dh])
        acc += _dot(o, wo_ref[sl, :])
    out_ref[...] = acc


def _xattn(x2d, kv, norm_w, wq_bf16, wo_bf16, s, n_mem, tm):
    t, d = x2d.shape
    tiles_per_seq = s // tm
    const = lambda shape: pl.BlockSpec(shape, lambda i: (0, 0))
    return pl.pallas_call(
        _xattn_kernel,
        grid=(t // tm,),
        in_specs=[pl.BlockSpec((tm, d), lambda i: (i, 0)),
                  pl.BlockSpec((n_mem, 2 * d), lambda i: (i // tiles_per_seq, 0)),
                  const((1, d)), const((d, d)), const((d, d))],
        out_specs=pl.BlockSpec((tm, d), lambda i: (i, 0)),
        out_shape=jax.ShapeDtypeStruct((t, d), F32),
        compiler_params=_cparams("parallel"),
        name="xattn",
    )(x2d, kv, norm_w.reshape(1, d), wq_bf16, wo_bf16)


def _top_rows(s, prio, payload, k):
    big = jnp.int32(2 ** 30)
    vals, pays = [], []
    for _ in range(k):
        m = jnp.max(s, axis=0, keepdims=True)
        pm = jnp.min(jnp.where(s == m, prio, big), axis=0, keepdims=True)
        sel = prio == pm
        vals.append(m)
        pays.append(pm if payload is None else jnp.sum(jnp.where(sel, payload, 0), axis=0, keepdims=True))
        s = jnp.where(sel, -jnp.inf, s)
    return vals, pays


def _peer_route_kernel(x_ref, nw_ref, wq_ref, k1_ref, k2_ref, h_ref, idx_ref, gate_ref):
    tm = x_ref.shape[0]
    k = PK_TOPK
    h = _rms_rows(x_ref[...], nw_ref[...])
    h_ref[...] = h
    q = _dot(h, wq_ref[...])
    key_iota = lax.broadcasted_iota(I32, (N_KEYS, tm), 0)
    r = lax.broadcasted_iota(I32, (80, 1), 0)
    prio = jnp.where(r < 16, r,
                     jnp.where(r < 72, (1 + ((r - 16) >> 3)) * k + ((r - 16) & 7), (r - 64) * k))
    idx_rows, gate_rows = [], []
    for hd in range(PK_HEADS):
        tops = []
        for half, kref in ((0, k1_ref), (1, k2_ref)):
            c0 = hd * 2 * N_KEYS + half * N_KEYS
            s_t = _dot_nt(kref[hd * N_KEYS:(hd + 1) * N_KEYS, :], q[:, c0:c0 + N_KEYS])
            vals, ids = _top_rows(s_t, key_iota, None, k)
            tops.append((jnp.concatenate(vals, axis=0), jnp.concatenate(ids, axis=0)))
        (v1, i1), (v2, i2) = tops
        cand_blocks = [v1[0:1] + v2]
        cidx_blocks = [i1[0:1] * N_KEYS + i2]
        for a in range(1, 8):
            cand_blocks.append(v1[a:a + 1] + v2[0:8])
            cidx_blocks.append(i1[a:a + 1] * N_KEYS + i2[0:8])
        cand_blocks.append(v1[8:16] + v2[0:1])
        cidx_blocks.append(i1[8:16] * N_KEYS + i2[0:1])
        cand = jnp.concatenate(cand_blocks, axis=0)
        cidx = jnp.concatenate(cidx_blocks, axis=0)
        sc, eidx = _top_rows(cand, prio, cidx, k)
        e = [jnp.exp(v - sc[0]) for v in sc]
        tot = functools.reduce(lambda a, b: a + b, e)
        gate_rows.extend([ek / tot for ek in e])
        idx_rows.extend(eidx)
    gate_ref[...] = jnp.concatenate(gate_rows, axis=0).T
    idx_ref[...] = jnp.concatenate(idx_rows, axis=0).T


def _peer_route(x2d, norm_w, wq_bf16, keys1_bf16, keys2_bf16, tm):
    t, d = x2d.shape
    const = lambda shape: pl.BlockSpec(shape, lambda i: (0, 0))
    return pl.pallas_call(
        _peer_route_kernel,
        grid=(t // tm,),
        in_specs=[pl.BlockSpec((tm, d), lambda i: (i, 0)), const((1, d)), const(wq_bf16.shape),
                  const(keys1_bf16.shape), const(keys2_bf16.shape)],
        out_specs=[pl.BlockSpec((tm, d), lambda i: (i, 0)),
                   pl.BlockSpec((tm, N_SEL), lambda i: (i, 0)),
                   pl.BlockSpec((tm, N_SEL), lambda i: (i, 0))],
        out_shape=[jax.ShapeDtypeStruct((t, d), F32),
                   jax.ShapeDtypeStruct((t, N_SEL), I32),
                   jax.ShapeDtypeStruct((t, N_SEL), F32)],
        compiler_params=_cparams("parallel"),
        name="peer_route",
    )(x2d, norm_w.reshape(1, d), wq_bf16, keys1_bf16, keys2_bf16)


def _peer_sc_body(n_tok, idx_hbm, gate_hbm, h_hbm, u_hbm, v_hbm, out_hbm,
                  idx_v, gate_v, x_v, out_v, u_buf, v_buf, tmp, cvec, sem_u, sem_v, sem_blk, sem_out):
    assert SC_U_AHEAD < SC_U_SLOTS and SC_V_AHEAD + 1 < SC_V_SLOTS
    assert SC_V_AHEAD <= SC_U_AHEAD < PK_HEADS
    d = h_hbm.shape[1]
    per_worker = n_tok // (SC_CORES * SC_SUBCORES)
    nblk = per_worker // SC_TOKB
    wid = lax.axis_index("s") * SC_CORES + lax.axis_index("c")
    ngrp = d // (2 * SC_LANES)
    lanes = lax.iota(I32, SC_LANES)
    zero = jnp.zeros((SC_LANES,), F32)

    def blk_copies(b, bs):
        rows = pl.ds(wid * per_worker + b * SC_TOKB, SC_TOKB)
        return (pltpu.make_async_copy(idx_hbm.at[rows], idx_v.at[bs], sem_blk.at[bs]),
                pltpu.make_async_copy(gate_hbm.at[rows], gate_v.at[bs], sem_blk.at[bs]),
                pltpu.make_async_copy(h_hbm.at[rows], x_v.at[bs], sem_blk.at[bs]))

    def out_copy(b, bs):
        rows = pl.ds(wid * per_worker + b * SC_TOKB, SC_TOKB)
        return pltpu.make_async_copy(out_v.at[bs], out_hbm.at[rows], sem_out.at[bs])

    def u_copy(bs, t, hd, slot):
        ids = idx_v.at[bs, t, pl.ds(hd * PK_TOPK, PK_TOPK)]
        return pltpu.make_async_copy(u_hbm.at[ids], u_buf.at[slot], sem_u.at[slot])

    def v_copy(bs, t, hd, slot):
        ids = idx_v.at[bs, t, pl.ds(hd * PK_TOPK, PK_TOPK)]
        return pltpu.make_async_copy(v_hbm.at[ids], v_buf.at[slot], sem_v.at[slot])

    def store_coef(accs, bs, t, hd):
        for e in range(PK_TOPK):
            tmp[e, :] = accs[e]
        a = zero
        for l in range(SC_LANES):
            a = a + plsc.load_gather(tmp, [lanes, (lanes + l) & (SC_LANES - 1)])
        z = 0.7978845608028654 * (a + 0.044715 * a * a * a)
        e2 = jnp.exp(-2.0 * jnp.abs(z))
        th = (1.0 - e2) / (1.0 + e2)
        th = jnp.where(z < 0.0, -th, th)
        cvec[pl.ds(SC_LANES, SC_LANES)] = gate_v[bs, t, pl.ds(hd * PK_TOPK, PK_TOPK)] * (0.5 * a * (1.0 + th))

    def load_coef():
        return [plsc.load_gather(cvec, [jnp.full((SC_LANES,), SC_LANES + e, I32)]) for e in range(PK_TOPK)]

    def mix_v(cb, vslot, j, o0, o1):
        for e in range(PK_TOPK):
            w = v_buf[vslot, e, pl.ds(j * SC_LANES, SC_LANES)]
            o0 = o0 + cb[e] * plsc.bitcast(w << 16, F32)
            o1 = o1 + cb[e] * plsc.bitcast(w & jnp.int32(-65536), F32)
        return o0, o1

    def dot_u_mix_v(bs, t, hd, uslot, vslot, pbs, pt, v_init):
        cb = load_coef()

        def step(j, accs):
            sl0 = pl.ds(j * 2 * SC_LANES, SC_LANES)
            sl1 = pl.ds(j * 2 * SC_LANES + SC_LANES, SC_LANES)
            x0 = x_v[bs, t, sl0]
            x1 = x_v[bs, t, sl1]
            new = tuple(accs[e] + u_buf[uslot, e, sl0] * x0 + u_buf[uslot, e, sl1] * x1 for e in range(PK_TOPK))
            o0, o1 = (zero, zero) if v_init else (out_v[pbs, pt, sl0], out_v[pbs, pt, sl1])
            o0, o1 = mix_v(cb, vslot, j, o0, o1)
            out_v[pbs, pt, sl0] = o0
            out_v[pbs, pt, sl1] = o1
            return new

        accs = plsc.parallel_loop(0, ngrp, carry=(zero,) * PK_TOPK)(step)
        store_coef(accs, bs, t, hd)

    def mix_v_only(vslot, pbs, pt):
        cb = load_coef()

        def step(j):
            sl0 = pl.ds(j * 2 * SC_LANES, SC_LANES)
            sl1 = pl.ds(j * 2 * SC_LANES + SC_LANES, SC_LANES)
            o0, o1 = mix_v(cb, vslot, j, out_v[pbs, pt, sl0], out_v[pbs, pt, sl1])
            out_v[pbs, pt, sl0] = o0
            out_v[pbs, pt, sl1] = o1

        plsc.parallel_loop(0, ngrp)(step)

    cvec[pl.ds(SC_LANES, SC_LANES)] = zero
    for e in range(PK_TOPK):
        def zero_v(j, e=e):
            v_buf[SC_V_SLOTS - 1, e, pl.ds(j * SC_LANES, SC_LANES)] = jnp.zeros((SC_LANES,), I32)
        plsc.parallel_loop(0, ngrp)(zero_v)

    def zero_out(j):
        out_v[1, SC_TOKB - 1, pl.ds(j * SC_LANES, SC_LANES)] = zero
    plsc.parallel_loop(0, d // SC_LANES)(zero_out)

    for cp in blk_copies(0, 0):
        cp.start()
    for cp in blk_copies(0, 0):
        cp.wait()
    for n in range(SC_U_AHEAD):
        u_copy(0, n // PK_HEADS, n % PK_HEADS, n % SC_U_SLOTS).start()
    for n in range(SC_V_AHEAD):
        v_copy(0, n // PK_HEADS, n % PK_HEADS, n % SC_V_SLOTS).start()

    def block(b, carry):
        bs = b % 2

        @pl.when(b + 1 < nblk)
        def _():
            for cp in blk_copies(b + 1, 1 - bs):
                cp.start()

        @pl.when(b >= 2)
        def _():
            out_copy(b - 2, bs).wait()

        def token(t, c2):
            g0 = (b * SC_TOKB + t) * PK_HEADS
            for hd in range(PK_HEADS):
                i = g0 + hd
                for mk, ahead, nslots in ((u_copy, SC_U_AHEAD, SC_U_SLOTS), (v_copy, SC_V_AHEAD, SC_V_SLOTS)):
                    slot_ahead = (i + ahead) % nslots
                    hn = hd + ahead
                    if hn < PK_HEADS:
                        mk(bs, t, hn, slot_ahead).start()
                    else:
                        hn -= PK_HEADS

                        @pl.when(t + 1 < SC_TOKB)
                        def _(mk=mk, hn=hn, slot_ahead=slot_ahead):
                            mk(bs, t + 1, hn, slot_ahead).start()

                        @pl.when(jnp.logical_and(t + 1 == SC_TOKB, b + 1 < nblk))
                        def _(mk=mk, hn=hn, slot_ahead=slot_ahead):
                            if mk is u_copy and hn == 0:
                                for cp in blk_copies(b + 1, 1 - bs):
                                    cp.wait()
                            mk(1 - bs, 0, hn, slot_ahead).start()
                uslot = i % SC_U_SLOTS
                vslot = (i + SC_V_SLOTS - 1) % SC_V_SLOTS
                u_copy(bs, t, hd, uslot).wait()
                if hd >= 1:
                    v_copy(bs, t, hd - 1, vslot).wait()
                    dot_u_mix_v(bs, t, hd, uslot, vslot, bs, t, v_init=(hd == 1))
                else:
                    pbs = jnp.where(t == 0, 1 - bs, bs)
                    pt = jnp.where(t == 0, SC_TOKB - 1, t - 1)

                    @pl.when(i >= 1)
                    def _():
                        v_copy(pbs, pt, PK_HEADS - 1, vslot).wait()

                    dot_u_mix_v(bs, t, hd, uslot, vslot, pbs, pt, v_init=False)

                    @pl.when(jnp.logical_and(t == 0, b >= 1))
                    def _():
                        out_copy(b - 1, 1 - bs).start()
            return c2

        lax.fori_loop(0, SC_TOKB, token, 0)
        return carry

    lax.fori_loop(0, nblk, block, 0)
    last = nblk * SC_TOKB * PK_HEADS - 1
    lbs = (nblk - 1) % 2
    v_copy(lbs, SC_TOKB - 1, PK_HEADS - 1, last % SC_V_SLOTS).wait()
    mix_v_only(last % SC_V_SLOTS, lbs, SC_TOKB - 1)
    out_copy(nblk - 1, lbs).start()
    for b in range(max(nblk - 2, 0), nblk):
        out_copy(b, b % 2).wait()


def _pack_bf16_pairs(v):
    n, d = v.shape
    bits = lax.bitcast_convert_type(v.astype(BF16), jnp.uint16).astype(jnp.uint32)
    bits = bits.reshape(n, d // (2 * SC_LANES), 2, SC_LANES)
    words = (bits[:, :, 1, :] << 16) | bits[:, :, 0, :]
    return lax.bitcast_convert_type(words.reshape(n, d // 2), I32)


def _pack_bf16_halves(v):
    n, d = v.shape
    bits = lax.bitcast_convert_type(v.astype(BF16), jnp.uint16).astype(jnp.uint32)
    words = (bits[:, d // 2:] << 16) | bits[:, :d // 2]
    return lax.bitcast_convert_type(words, F32)


def _peer_sc(idx, gate, h, u_tab, v_packed, n_tok):
    d = h.shape[1]
    assert n_tok % (SC_CORES * SC_SUBCORES * SC_TOKB) == 0 and d % (2 * SC_LANES) == 0
    mesh = plsc.VectorSubcoreMesh(core_axis_name="c", subcore_axis_name="s",
                                  num_cores=SC_CORES, num_subcores=SC_SUBCORES)
    return pl.kernel(
        functools.partial(_peer_sc_body, n_tok),
        out_type=jax.ShapeDtypeStruct((n_tok, d), F32),
        mesh=mesh,
        scratch_types=[
            pltpu.VMEM((2, SC_TOKB, N_SEL), I32),
            pltpu.VMEM((2, SC_TOKB, N_SEL), F32),
            pltpu.VMEM((2, SC_TOKB, d), F32),
            pltpu.VMEM((2, SC_TOKB, d), F32),
            pltpu.VMEM((SC_U_SLOTS, PK_TOPK, d), F32),
            pltpu.VMEM((SC_V_SLOTS, PK_TOPK, d // 2), I32),
            pltpu.VMEM((PK_TOPK, SC_LANES), F32),
            pltpu.VMEM((2 * SC_LANES,), F32),
            pltpu.SemaphoreType.DMA((SC_U_SLOTS,)),
            pltpu.SemaphoreType.DMA((SC_V_SLOTS,)),
            pltpu.SemaphoreType.DMA((2,)),
            pltpu.SemaphoreType.DMA((2,)),
        ],
        compiler_params=pltpu.CompilerParams(needs_layout_passes=False),
        name="peer_sc",
    )(idx, gate, h, u_tab, v_packed)


def _final_kernel(x_ref, p_ref, nw_ref, *rest):
    out_ref = rest[-1]
    out_ref[...] = _rms_rows(x_ref[...] + p_ref[...], nw_ref[...])


def _final_norm(x2d, peer_out, norm_w, y_full, row0, t_full, tm):
    t, d = peer_out.shape
    assert row0 % tm == 0 and t % tm == 0
    blk0 = row0 // tm
    row = pl.BlockSpec((tm, d), lambda i: (i, 0))
    in_specs = [row, row, pl.BlockSpec((1, d), lambda i: (0, 0))]
    operands = [x2d, peer_out, norm_w.reshape(1, d)]
    aliases = {}
    if y_full is not None:
        in_specs.append(pl.BlockSpec(memory_space=pl.ANY))
        operands.append(y_full)
        aliases = {3: 0}
    return pl.pallas_call(
        _final_kernel,
        grid=(t // tm,),
        in_specs=in_specs,
        out_specs=pl.BlockSpec((tm, d), lambda i: (i + blk0, 0)),
        out_shape=jax.ShapeDtypeStruct((t_full, d), F32),
        input_output_aliases=aliases,
        compiler_params=_cparams("parallel"),
        name="final_norm",
    )(*operands)


def _gelu_tanh(x):
    return 0.5 * x * (1.0 + jnp.tanh(0.7978845608028654 * (x + 0.044715 * x * x * x)))


def _peer_tc_kernel(idx_hbm, uv_hbm, h_ref, gate_ref, x_ref, nw_ref, y_any, out_ref,
                    idx_smem, rows, acc_ref, gate_t_ref, idx_sem, row_sem):
    del y_any
    tb, d = h_ref.shape
    blk = pl.program_id(0)

    idx_copy = pltpu.make_async_copy(idx_hbm.at[pl.ds(blk * (tb * N_SEL), tb * N_SEL)], idx_smem, idx_sem)
    idx_copy.start()
    idx_copy.wait()

    def start_rows(tok, slot):
        for e in range(N_SEL):
            pltpu.make_async_copy(uv_hbm.at[pl.ds(idx_smem[tok * N_SEL + e], 1)],
                                  rows.at[slot, pl.ds(e, 1)], row_sem.at[slot]).start()

    def wait_rows(slot):
        pltpu.make_async_copy(uv_hbm.at[pl.ds(0, N_SEL)], rows.at[slot], row_sem.at[slot]).wait()

    lane = lax.broadcasted_iota(I32, (N_SEL, tb), 1)
    gate_t_ref[...] = gate_ref[...].T

    for j in range(TC_ROW_SLOTS - 1):
        start_rows(j, j)

    def body(j, carry):
        slot = j % TC_ROW_SLOTS
        ahead = j + TC_ROW_SLOTS - 1

        @pl.when(ahead < tb)
        def _():
            start_rows(ahead, ahead % TC_ROW_SLOTS)

        wait_rows(slot)
        x_row = h_ref[pl.ds(j, 1), :]
        u = rows[slot, :, 0:d]
        a = jnp.sum(u * x_row, axis=1, keepdims=True)
        g = jnp.sum(jnp.where(lane == j, gate_t_ref[...], 0.0), axis=1, keepdims=True)
        coef = g * _gelu_tanh(a)
        w = lax.bitcast_convert_type(rows[slot, :, d:d + d // 2], I32)
        v_lo = lax.bitcast_convert_type(w << 16, F32)
        v_hi = lax.bitcast_convert_type(w & jnp.int32(-65536), F32)
        acc_ref[pl.ds(j, 1), 0:d // 2] = jnp.sum(coef * v_lo, axis=0, keepdims=True)
        acc_ref[pl.ds(j, 1), d // 2:d] = jnp.sum(coef * v_hi, axis=0, keepdims=True)
        return carry

    lax.fori_loop(0, tb, body, 0)
    out_ref[...] = _rms_rows(x_ref[...] + acc_ref[...], nw_ref[...])


def _peer_tc(idx_flat, uv, h, gate, x2d, norm_w, y_full, row0, t_full, tb):
    t, d = x2d.shape
    assert tb == N_SEL and t % tb == 0 and row0 % tb == 0 and y_full.shape == (t_full, d)
    blk0 = row0 // tb
    row = pl.BlockSpec((tb, d), lambda i: (i, 0))
    return pl.pallas_call(
        _peer_tc_kernel,
        grid=(t // tb,),
        in_specs=[pl.BlockSpec(memory_space=pl.ANY), pl.BlockSpec(memory_space=pl.ANY),
                  row, pl.BlockSpec((tb, N_SEL), lambda i: (i, 0)), row,
                  pl.BlockSpec((1, d), lambda i: (0, 0)), pl.BlockSpec(memory_space=pl.ANY)],
        out_specs=pl.BlockSpec((tb, d), lambda i: (i + blk0, 0)),
        out_shape=jax.ShapeDtypeStruct((t_full, d), F32),
        input_output_aliases={6: 0},
        scratch_shapes=[pltpu.SMEM((tb * N_SEL,), I32),
                        pltpu.VMEM((TC_ROW_SLOTS, N_SEL, d + d // 2), F32),
                        pltpu.VMEM((tb, d), F32),
                        pltpu.VMEM((N_SEL, tb), F32),
                        pltpu.SemaphoreType.DMA(()),
                        pltpu.SemaphoreType.DMA((TC_ROW_SLOTS,))],
        compiler_params=_cparams("arbitrary"),
        name="peer_tc",
    )(idx_flat, uv, h, gate, x2d, norm_w.reshape(1, d), y_full)


def _dense_stages(x, mem, p):
    bn, s, d = x.shape
    n_mem = mem.shape[1]
    x2d = x.reshape(bn * s, d)

    z = _rms_matmul(x2d, p["norm_mix_w"], p["w_in"], tm=256)
    hf, hb = _mlstm_scan(z, p["gate_b"], bn, s)
    x1 = _mixer_out(x2d, z, hf, hb, p["conv_w"], p["conv_b"], p["mlstm_norm_w"], p["w_out"], s, tm=256)

    kv = _rms_matmul(mem.reshape(bn * n_mem, d), p["norm_mem_w"], p["xa_wkv"], tm=256)
    x2 = _xattn(x1, kv, p["norm_xattn_w"], p["xa_wq"], p["xa_wo"], s, n_mem, tm=256)

    h3, eidx, gate = _peer_route(x2, p["norm_ffn_w"], p["peer_wq"], p["peer_keys1"], p["peer_keys2"], tm=256)
    return x2, h3, eidx, gate


def _layer(groups, p):
    assert 0 < TC_PIECES < groups[0][0].shape[0] * groups[0][0].shape[1] // PIECE_TOKENS
    sc_pieces = [[] for _ in groups]
    handle = last_route = None
    tc_pieces = []
    for gi, (x, mem) in enumerate(groups):
        bn, s, d = x.shape
        rows = max(1, min(bn, PIECE_TOKENS // s))
        assert bn % rows == 0
        if handle is not None:
            x, _ = lax.optimization_barrier((x, handle))
        for i in range(0, bn, rows):
            x2, h3, eidx, gate = _dense_stages(x[i:i + rows], mem[i:i + rows], p)
            handle = eidx if i == 0 else handle
            last_route = eidx
            if gi == 0 and i + rows * TC_PIECES >= bn:
                tc_pieces.append((eidx, gate, h3, x2, i * s))
            else:
                peer_out = _peer_sc(eidx, gate, h3, p["peer_u"], p["peer_v_packed"], rows * s)
                sc_pieces[gi].append((x2, peer_out, i * s))
    outs = []
    for gi, (x, _) in enumerate(groups):
        t_full = x.shape[0] * x.shape[1]
        y = None
        for x2, peer_out, row0 in sc_pieces[gi]:
            y = _final_norm(x2, peer_out, p["norm_final_w"], y, row0, t_full, tm=512)
        for eidx, gate, h3, x2, row0 in (tc_pieces if gi == 0 else []):
            eidx, _ = lax.optimization_barrier((eidx, last_route))
            y = _peer_tc(eidx.reshape(-1), p["peer_uv"], h3, gate, x2, p["norm_final_w"], y, row0, t_full, tb=N_SEL)
        outs.append(y.reshape(x.shape))
    return outs


def kernel(x_prompt, x_sample, mem_prompt, mem_sample, norm_mix_w, w_in, gate_b, conv_w, conv_b, mlstm_norm_w, w_out, norm_xattn_w, norm_mem_w, xa_wq, xa_wk, xa_wv, xa_wo, norm_ffn_w, peer_wq, peer_keys1, peer_keys2, peer_u, peer_v, norm_final_w):
    assert w_in.shape[0] == 1, "single-layer trunk"
    d = x_prompt.shape[-1]
    gate_pad = LANES - GATE_COLS
    p = {
        "norm_mix_w": norm_mix_w[0],
        "w_in": jnp.pad(w_in[0], ((0, 0), (0, gate_pad))).astype(BF16),
        "gate_b": jnp.pad(gate_b[0], (0, gate_pad)).reshape(1, LANES),
        "conv_w": conv_w[0], "conv_b": conv_b[0], "mlstm_norm_w": mlstm_norm_w[0],
        "w_out": w_out[0].astype(BF16),
        "norm_xattn_w": norm_xattn_w[0], "norm_mem_w": norm_mem_w[0],
        "xa_wq": xa_wq[0].astype(BF16),
        "xa_wkv": jnp.concatenate([xa_wk[0], xa_wv[0]], axis=1).astype(BF16),
        "xa_wo": xa_wo[0].astype(BF16),
        "norm_ffn_w": norm_ffn_w[0],
        "peer_wq": peer_wq[0].astype(BF16),
        "peer_keys1": peer_keys1[0].reshape(PK_HEADS * N_KEYS, -1).astype(BF16),
        "peer_keys2": peer_keys2[0].reshape(PK_HEADS * N_KEYS, -1).astype(BF16),
        "peer_u": peer_u[0], "peer_v_packed": _pack_bf16_pairs(peer_v[0]),
        "peer_uv": jnp.concatenate([peer_u[0], _pack_bf16_halves(peer_v[0])], axis=1),
        "norm_final_w": norm_final_w,
    }
    assert p["w_in"].shape[1] == Z_MAIN + LANES and d == 1024
    groups = {"prompt": (x_prompt, mem_prompt), "sample": (x_sample, mem_sample)}
    order = sorted(groups, key=lambda g: groups[g][0].shape[1])
    out = dict(zip(order, _layer([groups[g] for g in order], p)))
    return (out["prompt"], out["sample"])
```

```python
import functools

import jax
import jax.numpy as jnp
from jax import lax
from jax.experimental import pallas as pl
from jax.experimental.pallas import tpu as pltpu
from jax.experimental.pallas import tpu_sc as plsc

F32 = jnp.float32
BF16 = jnp.bfloat16
I32 = jnp.int32

EPS = 1e-6
M_INIT = -1e30
LANES = 128
SUBLANES = 8

CONV_W = 512
ML_HEADS = 4
ML_DH = 128
ML_W = ML_HEADS * ML_DH
CHUNK = 128
GATE_COLS = 4 * ML_HEADS
XA_HEADS = 4
N_KEYS = 128
PK_HEADS = 8
PK_TOPK = 16
N_SEL = PK_HEADS * PK_TOPK
VMEM_LIMIT = 56 * 1024 * 1024
SC_CORES = 2
SC_SUBCORES = 16
SC_LANES = 16
SC_TOKB = 4
SC_U_SLOTS = 4
SC_V_SLOTS = 4
SC_U_AHEAD = 3
SC_V_AHEAD = 2
PIECE_TOKENS = 4096
TC_TOKENS = 14336
TC_ROW_SLOTS = 4

ZB_XT, ZB_BG, ZB_CG, ZB_Q, ZB_K, ZB_V, ZB_O = range(7)
Z_MAIN = 7 * 512
ZB_GATES = Z_MAIN // LANES


def _cparams(*sem):
    return pltpu.CompilerParams(dimension_semantics=sem, vmem_limit_bytes=VMEM_LIMIT)


def _rms_rows(x, w):
    return x * lax.rsqrt(jnp.mean(x * x, axis=-1, keepdims=True) + EPS) * w


def _log_sigmoid(x):
    return jnp.minimum(x, 0.0) - jnp.log(1.0 + jnp.exp(-jnp.abs(x)))


def _dot(a, b):
    return jnp.dot(a.astype(BF16), b.astype(BF16), preferred_element_type=F32)


def _dot_nt(a, b):
    return lax.dot_general(a.astype(BF16), b.astype(BF16), (((1,), (1,)), ((), ())),
                           preferred_element_type=F32)


def _dot_tn(a, b):
    return lax.dot_general(a.astype(BF16), b.astype(BF16), (((0,), (0,)), ((), ())),
                           preferred_element_type=F32)


def _rms_matmul_kernel(x_ref, nw_ref, w_ref, o_ref):
    h = _rms_rows(x_ref[...], nw_ref[...])
    o_ref[...] = _dot(h, w_ref[...])


def _rms_matmul(x2d, norm_w, w_bf16, tm):
    t, d = x2d.shape
    n = w_bf16.shape[1]
    return pl.pallas_call(
        _rms_matmul_kernel,
        grid=(t // tm,),
        in_specs=[pl.BlockSpec((tm, d), lambda i: (i, 0)),
                  pl.BlockSpec((1, d), lambda i: (0, 0)),
                  pl.BlockSpec((d, n), lambda i: (0, 0))],
        out_specs=pl.BlockSpec((tm, n), lambda i: (i, 0)),
        out_shape=jax.ShapeDtypeStruct((t, n), F32),
        compiler_params=_cparams("parallel"),
        name="rms_matmul",
    )(x2d, norm_w.reshape(1, d), w_bf16)


def _mlstm_direction(d, q_ref, k_ref, v_ref, g_ref, gb_ref, h_ref, c_ref, n_ref, m_ref):
    L = CHUNK
    r = lax.broadcasted_iota(I32, (L, L), 0)
    c = lax.broadcasted_iota(I32, (L, L), 1)
    allowed = (c <= r) if d == 0 else (c >= r)
    tri = allowed.astype(F32)
    g_all = g_ref[...] + gb_ref[...]
    lf = _log_sigmoid(g_all)
    a_all = jnp.dot(tri, lf, precision=lax.Precision.HIGHEST, preferred_element_type=F32)
    a_all_t = a_all.T
    g_all_t = g_all.T
    end = L - 1 if d == 0 else 0
    for hd in range(ML_HEADS):
        ci = d * ML_HEADS + hd
        cf = 2 * ML_HEADS + d * ML_HEADS + hd
        st = d * ML_HEADS + hd
        sl = slice(hd * ML_DH, (hd + 1) * ML_DH)
        q = q_ref[:, sl] * (ML_DH ** -0.5)
        k = k_ref[:, sl]
        v = v_ref[:, sl]
        a_col = a_all[:, cf:cf + 1]
        a_row = a_all_t[cf:cf + 1, :]
        ig_col = g_all[:, ci:ci + 1]
        ig_row = g_all_t[ci:ci + 1, :]
        g_tot = a_col[end:end + 1, :]
        c_prev = c_ref[st]
        n_prev = n_ref[st]
        m_prev = m_ref[st]

        dlog = jnp.where(allowed, a_col - a_row + ig_row, -jnp.inf)
        m_inter = a_col + m_prev
        m_t = jnp.maximum(m_inter, jnp.max(dlog, axis=1, keepdims=True))
        s = jnp.exp(dlog - m_t) * _dot_nt(q, k)
        e_inter = jnp.exp(m_inter - m_t)
        num = _dot(s, v) + e_inter * _dot(q, c_prev)
        den = jnp.sum(s, axis=1, keepdims=True) + e_inter * jnp.sum(q * n_prev, axis=1, keepdims=True)
        h_ref[:, sl] = num / jnp.maximum(jnp.abs(den), jnp.exp(-m_t))

        w_col = g_tot - a_col + ig_col
        m_new = jnp.maximum(g_tot + m_prev, jnp.max(w_col, axis=0, keepdims=True))
        decay = jnp.exp(g_tot + m_prev - m_new)
        ksc = k * jnp.exp(w_col - m_new)
        c_ref[st] = decay * c_prev + _dot_tn(ksc, v)
        n_ref[st] = decay * n_prev + jnp.sum(ksc, axis=0, keepdims=True)
        m_ref[st] = m_new


def _mlstm_scan_kernel(qf, kf, vf, gf, qb, kb, vb, gbk, gb_ref, hf_ref, hb_ref, c_ref, n_ref, m_ref):
    @pl.when(pl.program_id(1) == 0)
    def _():
        c_ref[...] = jnp.zeros_like(c_ref)
        n_ref[...] = jnp.zeros_like(n_ref)
        m_ref[...] = jnp.full_like(m_ref, M_INIT)

    _mlstm_direction(0, qf, kf, vf, gf, gb_ref, hf_ref, c_ref, n_ref, m_ref)
    _mlstm_direction(1, qb, kb, vb, gbk, gb_ref, hb_ref, c_ref, n_ref, m_ref)


def _mlstm_scan(z, gate_b_pad, bn, s):
    nc = s // CHUNK
    t = bn * s

    def fwd(col):
        return lambda b, i: (b * nc + i, col)

    def bwd(col):
        return lambda b, i: (b * nc + nc - 1 - i, col)

    wide = lambda im: pl.BlockSpec((CHUNK, 512), im)
    gate = lambda im: pl.BlockSpec((CHUNK, LANES), im)
    n_state = 2 * ML_HEADS
    return pl.pallas_call(
        _mlstm_scan_kernel,
        grid=(bn, nc),
        in_specs=[wide(fwd(ZB_Q)), wide(fwd(ZB_K)), wide(fwd(ZB_V)), gate(fwd(ZB_GATES)),
                  wide(bwd(ZB_Q)), wide(bwd(ZB_K)), wide(bwd(ZB_V)), gate(bwd(ZB_GATES)),
                  pl.BlockSpec((1, LANES), lambda b, i: (0, 0))],
        out_specs=[wide(fwd(0)), wide(bwd(0))],
        out_shape=[jax.ShapeDtypeStruct((t, ML_W), F32), jax.ShapeDtypeStruct((t, ML_W), F32)],
        scratch_shapes=[pltpu.VMEM((n_state, ML_DH, ML_DH), F32),
                        pltpu.VMEM((n_state, 1, ML_DH), F32),
                        pltpu.VMEM((n_state, 1, 1), F32)],
        compiler_params=_cparams("parallel", "arbitrary"),
        name="mlstm_scan",
    )(z, z, z, z, z, z, z, z, gate_b_pad)


def _mixer_out_kernel(tiles_per_seq, x_ref, xt_ref, bg_ref, cg_ref, o_ref, hf_ref, hb_ref,
                      xt_p, cg_p, xt_n, cg_n, cw_ref, cb_ref, mw_ref, wo_ref, out_ref):
    tm = x_ref.shape[0]
    pos = pl.program_id(0) % tiles_per_seq
    u = cg_ref[...] * xt_ref[...]
    has_prev = (pos > 0).astype(F32)
    has_next = (pos < tiles_per_seq - 1).astype(F32)
    u_before = (cg_p[...] * xt_p[...])[SUBLANES - 1:SUBLANES, :] * has_prev
    u_after = (cg_n[...] * xt_n[...])[0:1, :] * has_next
    row = lax.broadcasted_iota(I32, u.shape, 0)
    u_prev = jnp.where(row == 0, u_before, pltpu.roll(u, 1, axis=0))
    u_next = jnp.where(row == tm - 1, u_after, pltpu.roll(u, tm - 1, axis=0))
    conv = cw_ref[0:1, :] * u_prev + cw_ref[1:2, :] * u + cw_ref[2:3, :] * u_next + cb_ref[...]
    y_conv = bg_ref[...] * conv

    hm = hf_ref[...] + hb_ref[...]
    og = jax.nn.sigmoid(o_ref[...])
    acc = x_ref[...] + _dot(y_conv, wo_ref[0:CONV_W, :])
    for hd in range(ML_HEADS):
        sl = slice(hd * ML_DH, (hd + 1) * ML_DH)
        y_ml = og[:, sl] * _rms_rows(hm[:, sl], mw_ref[:, sl])
        acc += _dot(y_ml, wo_ref[CONV_W + hd * ML_DH:CONV_W + (hd + 1) * ML_DH, :])
    out_ref[...] = acc


def _mixer_out(x2d, z, hf, hb, conv_w, conv_b, ml_norm_w, w_out_bf16, s, tm):
    t, d = x2d.shape
    tiles_per_seq = s // tm
    rb = tm // SUBLANES
    last_rb = t // SUBLANES - 1
    wide = lambda col: pl.BlockSpec((tm, 512), lambda i: (i, col))
    halo_prev = lambda col: pl.BlockSpec((SUBLANES, 512), lambda i: (jnp.maximum(i * rb - 1, 0), col))
    halo_next = lambda col: pl.BlockSpec((SUBLANES, 512), lambda i: (jnp.minimum((i + 1) * rb, last_rb), col))
    const = lambda shape: pl.BlockSpec(shape, lambda i: (0, 0))
    return pl.pallas_call(
        functools.partial(_mixer_out_kernel, tiles_per_seq),
        grid=(t // tm,),
        in_specs=[pl.BlockSpec((tm, d), lambda i: (i, 0)),
                  wide(ZB_XT), wide(ZB_BG), wide(ZB_CG), wide(ZB_O),
                  wide(0), wide(0),
                  halo_prev(ZB_XT), halo_prev(ZB_CG), halo_next(ZB_XT), halo_next(ZB_CG),
                  const((3, CONV_W)), const((1, CONV_W)), const((1, ML_W)), const((d, d))],
        out_specs=pl.BlockSpec((tm, d), lambda i: (i, 0)),
        out_shape=jax.ShapeDtypeStruct((t, d), F32),
        compiler_params=_cparams("parallel"),
        name="mixer_out",
    )(x2d, z, z, z, z, hf, hb, z, z, z, z, conv_w, conv_b.reshape(1, -1), ml_norm_w.reshape(1, -1), w_out_bf16)


def _xattn_kernel(x_ref, kv_ref, nw_ref, wq_ref, wo_ref, out_ref):
    x = x_ref[...]
    d = x.shape[1]
    dh = d // XA_HEADS
    q = _dot(_rms_rows(x, nw_ref[...]), wq_ref[...])
    acc = x
    for hd in range(XA_HEADS):
        sl = slice(hd * dh, (hd + 1) * dh)
        s = _dot_nt(q[:, sl], kv_ref[:, sl]) * (dh ** -0.5)
        s = s - jnp.max(s, axis=-1, keepdims=True)
        p = jnp.exp(s)
        p = p / jnp.sum(p, axis=-1, keepdims=True)
        o = _dot(p, kv_ref[:, d + hd * dh:d + (hd + 1) * dh])
        acc += _dot(o, wo_ref[sl, :])
    out_ref[...] = acc


def _xattn(x2d, kv, norm_w, wq_bf16, wo_bf16, s, n_mem, tm):
    t, d = x2d.shape
    tiles_per_seq = s // tm
    const = lambda shape: pl.BlockSpec(shape, lambda i: (0, 0))
    return pl.pallas_call(
        _xattn_kernel,
        grid=(t // tm,),
        in_specs=[pl.BlockSpec((tm, d), lambda i: (i, 0)),
                  pl.BlockSpec((n_mem, 2 * d), lambda i: (i // tiles_per_seq, 0)),
                  const((1, d)), const((d, d)), const((d, d))],
        out_specs=pl.BlockSpec((tm, d), lambda i: (i, 0)),
        out_shape=jax.ShapeDtypeStruct((t, d), F32),
        compiler_params=_cparams("parallel"),
        name="xattn",
    )(x2d, kv, norm_w.reshape(1, d), wq_bf16, wo_bf16)


def _top_rows(s, prio, payload, k):
    big = jnp.int32(2 ** 30)
    vals, pays = [], []
    for _ in range(k):
        m = jnp.max(s, axis=0, keepdims=True)
        pm = jnp.min(jnp.where(s == m, prio, big), axis=0, keepdims=True)
        sel = prio == pm
        vals.append(m)
        pays.append(pm if payload is None else jnp.sum(jnp.where(sel, payload, 0), axis=0, keepdims=True))
        s = jnp.where(sel, -jnp.inf, s)
    return vals, pays


def _peer_route_kernel(x_ref, nw_ref, wq_ref, k1_ref, k2_ref, h_ref, idx_ref, gate_ref):
    tm = x_ref.shape[0]
    k = PK_TOPK
    h = _rms_rows(x_ref[...], nw_ref[...])
    h_ref[...] = h
    q = _dot(h, wq_ref[...])
    key_iota = lax.broadcasted_iota(I32, (N_KEYS, tm), 0)
    r = lax.broadcasted_iota(I32, (80, 1), 0)
    prio = jnp.where(r < 16, r,
                     jnp.where(r < 72, (1 + ((r - 16) >> 3)) * k + ((r - 16) & 7), (r - 64) * k))
    idx_rows, gate_rows = [], []
    for hd in range(PK_HEADS):
        tops = []
        for half, kref in ((0, k1_ref), (1, k2_ref)):
            c0 = hd * 2 * N_KEYS + half * N_KEYS
            s_t = _dot_nt(kref[hd * N_KEYS:(hd + 1) * N_KEYS, :], q[:, c0:c0 + N_KEYS])
            vals, ids = _top_rows(s_t, key_iota, None, k)
            tops.append((jnp.concatenate(vals, axis=0), jnp.concatenate(ids, axis=0)))
        (v1, i1), (v2, i2) = tops
        cand_blocks = [v1[0:1] + v2]
        cidx_blocks = [i1[0:1] * N_KEYS + i2]
        for a in range(1, 8):
            cand_blocks.append(v1[a:a + 1] + v2[0:8])
            cidx_blocks.append(i1[a:a + 1] * N_KEYS + i2[0:8])
        cand_blocks.append(v1[8:16] + v2[0:1])
        cidx_blocks.append(i1[8:16] * N_KEYS + i2[0:1])
        cand = jnp.concatenate(cand_blocks, axis=0)
        cidx = jnp.concatenate(cidx_blocks, axis=0)
        sc, eidx = _top_rows(cand, prio, cidx, k)
        e = [jnp.exp(v - sc[0]) for v in sc]
        tot = functools.reduce(lambda a, b: a + b, e)
        gate_rows.extend([ek / tot for ek in e])
        idx_rows.extend(eidx)
    gate_ref[...] = jnp.concatenate(gate_rows, axis=0).T
    idx_ref[...] = jnp.concatenate(idx_rows, axis=0).T


def _peer_route(x2d, norm_w, wq_bf16, keys1_bf16, keys2_bf16, tm):
    t, d = x2d.shape
    const = lambda shape: pl.BlockSpec(shape, lambda i: (0, 0))
    return pl.pallas_call(
        _peer_route_kernel,
        grid=(t // tm,),
        in_specs=[pl.BlockSpec((tm, d), lambda i: (i, 0)), const((1, d)), const(wq_bf16.shape),
                  const(keys1_bf16.shape), const(keys2_bf16.shape)],
        out_specs=[pl.BlockSpec((tm, d), lambda i: (i, 0)),
                   pl.BlockSpec((tm, N_SEL), lambda i: (i, 0)),
                   pl.BlockSpec((tm, N_SEL), lambda i: (i, 0))],
        out_shape=[jax.ShapeDtypeStruct((t, d), F32),
                   jax.ShapeDtypeStruct((t, N_SEL), I32),
                   jax.ShapeDtypeStruct((t, N_SEL), F32)],
        compiler_params=_cparams("parallel"),
        name="peer_route",
    )(x2d, norm_w.reshape(1, d), wq_bf16, keys1_bf16, keys2_bf16)


def _peer_sc_body(n_tok, idx_hbm, gate_hbm, h_hbm, u_hbm, v_hbm, out_hbm,
                  idx_v, gate_v, x_v, out_v, u_buf, v_buf, tmp, cvec, sem_u, sem_v, sem_blk, sem_out):
    assert SC_U_AHEAD < SC_U_SLOTS and SC_V_AHEAD + 1 < SC_V_SLOTS
    assert SC_V_AHEAD <= SC_U_AHEAD < PK_HEADS
    d = h_hbm.shape[1]
    per_worker = n_tok // (SC_CORES * SC_SUBCORES)
    nblk = per_worker // SC_TOKB
    wid = lax.axis_index("s") * SC_CORES + lax.axis_index("c")
    ngrp = d // (2 * SC_LANES)
    lanes = lax.iota(I32, SC_LANES)
    zero = jnp.zeros((SC_LANES,), F32)

    def blk_copies(b, bs):
        rows = pl.ds(wid * per_worker + b * SC_TOKB, SC_TOKB)
        return (pltpu.make_async_copy(idx_hbm.at[rows], idx_v.at[bs], sem_blk.at[bs]),
                pltpu.make_async_copy(gate_hbm.at[rows], gate_v.at[bs], sem_blk.at[bs]),
                pltpu.make_async_copy(h_hbm.at[rows], x_v.at[bs], sem_blk.at[bs]))

    def out_copy(b, bs):
        rows = pl.ds(wid * per_worker + b * SC_TOKB, SC_TOKB)
        return pltpu.make_async_copy(out_v.at[bs], out_hbm.at[rows], sem_out.at[bs])

    def u_copy(bs, t, hd, slot):
        ids = idx_v.at[bs, t, pl.ds(hd * PK_TOPK, PK_TOPK)]
        return pltpu.make_async_copy(u_hbm.at[ids], u_buf.at[slot], sem_u.at[slot])

    def v_copy(bs, t, hd, slot):
        ids = idx_v.at[bs, t, pl.ds(hd * PK_TOPK, PK_TOPK)]
        return pltpu.make_async_copy(v_hbm.at[ids], v_buf.at[slot], sem_v.at[slot])

    def store_coef(accs, bs, t, hd):
        for e in range(PK_TOPK):
            tmp[e, :] = accs[e]
        a = zero
        for l in range(SC_LANES):
            a = a + plsc.load_gather(tmp, [lanes, (lanes + l) & (SC_LANES - 1)])
        z = 0.7978845608028654 * (a + 0.044715 * a * a * a)
        e2 = jnp.exp(-2.0 * jnp.abs(z))
        th = (1.0 - e2) / (1.0 + e2)
        th = jnp.where(z < 0.0, -th, th)
        cvec[pl.ds(SC_LANES, SC_LANES)] = gate_v[bs, t, pl.ds(hd * PK_TOPK, PK_TOPK)] * (0.5 * a * (1.0 + th))

    def load_coef():
        return [plsc.load_gather(cvec, [jnp.full((SC_LANES,), SC_LANES + e, I32)]) for e in range(PK_TOPK)]

    def mix_v(cb, vslot, j, o0, o1):
        for e in range(PK_TOPK):
            w = v_buf[vslot, e, pl.ds(j * SC_LANES, SC_LANES)]
            o0 = o0 + cb[e] * plsc.bitcast(w << 16, F32)
            o1 = o1 + cb[e] * plsc.bitcast(w & jnp.int32(-65536), F32)
        return o0, o1

    def dot_u_mix_v(bs, t, hd, uslot, vslot, pbs, pt, v_init):
        cb = load_coef()

        def step(j, accs):
            sl0 = pl.ds(j * 2 * SC_LANES, SC_LANES)
            sl1 = pl.ds(j * 2 * SC_LANES + SC_LANES, SC_LANES)
            x0 = x_v[bs, t, sl0]
            x1 = x_v[bs, t, sl1]
            new = tuple(accs[e] + u_buf[uslot, e, sl0] * x0 + u_buf[uslot, e, sl1] * x1 for e in range(PK_TOPK))
            o0, o1 = (zero, zero) if v_init else (out_v[pbs, pt, sl0], out_v[pbs, pt, sl1])
            o0, o1 = mix_v(cb, vslot, j, o0, o1)
            out_v[pbs, pt, sl0] = o0
            out_v[pbs, pt, sl1] = o1
            return new

        accs = plsc.parallel_loop(0, ngrp, carry=(zero,) * PK_TOPK)(step)
        store_coef(accs, bs, t, hd)

    def mix_v_only(vslot, pbs, pt):
        cb = load_coef()

        def step(j):
            sl0 = pl.ds(j * 2 * SC_LANES, SC_LANES)
            sl1 = pl.ds(j * 2 * SC_LANES + SC_LANES, SC_LANES)
            o0, o1 = mix_v(cb, vslot, j, out_v[pbs, pt, sl0], out_v[pbs, pt, sl1])
            out_v[pbs, pt, sl0] = o0
            out_v[pbs, pt, sl1] = o1

        plsc.parallel_loop(0, ngrp)(step)

    cvec[pl.ds(SC_LANES, SC_LANES)] = zero
    for e in range(PK_TOPK):
        def zero_v(j, e=e):
            v_buf[SC_V_SLOTS - 1, e, pl.ds(j * SC_LANES, SC_LANES)] = jnp.zeros((SC_LANES,), I32)
        plsc.parallel_loop(0, ngrp)(zero_v)

    def zero_out(j):
        out_v[1, SC_TOKB - 1, pl.ds(j * SC_LANES, SC_LANES)] = zero
    plsc.parallel_loop(0, d // SC_LANES)(zero_out)

    for cp in blk_copies(0, 0):
        cp.start()
    for cp in blk_copies(0, 0):
        cp.wait()
    for n in range(SC_U_AHEAD):
        u_copy(0, n // PK_HEADS, n % PK_HEADS, n % SC_U_SLOTS).start()
    for n in range(SC_V_AHEAD):
        v_copy(0, n // PK_HEADS, n % PK_HEADS, n % SC_V_SLOTS).start()

    def block(b, carry):
        bs = b % 2

        @pl.when(b + 1 < nblk)
        def _():
            for cp in blk_copies(b + 1, 1 - bs):
                cp.start()

        @pl.when(b >= 2)
        def _():
            out_copy(b - 2, bs).wait()

        def token(t, c2):
            g0 = (b * SC_TOKB + t) * PK_HEADS
            for hd in range(PK_HEADS):
                i = g0 + hd
                for mk, ahead, nslots in ((u_copy, SC_U_AHEAD, SC_U_SLOTS), (v_copy, SC_V_AHEAD, SC_V_SLOTS)):
                    slot_ahead = (i + ahead) % nslots
                    hn = hd + ahead
                    if hn < PK_HEADS:
                        mk(bs, t, hn, slot_ahead).start()
                    else:
                        hn -= PK_HEADS

                        @pl.when(t + 1 < SC_TOKB)
                        def _(mk=mk, hn=hn, slot_ahead=slot_ahead):
                            mk(bs, t + 1, hn, slot_ahead).start()

                        @pl.when(jnp.logical_and(t + 1 == SC_TOKB, b + 1 < nblk))
                        def _(mk=mk, hn=hn, slot_ahead=slot_ahead):
                            if mk is u_copy and hn == 0:
                                for cp in blk_copies(b + 1, 1 - bs):
                                    cp.wait()
                            mk(1 - bs, 0, hn, slot_ahead).start()
                uslot = i % SC_U_SLOTS
                vslot = (i + SC_V_SLOTS - 1) % SC_V_SLOTS
                u_copy(bs, t, hd, uslot).wait()
                if hd >= 1:
                    v_copy(bs, t, hd - 1, vslot).wait()
                    dot_u_mix_v(bs, t, hd, uslot, vslot, bs, t, v_init=(hd == 1))
                else:
                    pbs = jnp.where(t == 0, 1 - bs, bs)
                    pt = jnp.where(t == 0, SC_TOKB - 1, t - 1)

                    @pl.when(i >= 1)
                    def _():
                        v_copy(pbs, pt, PK_HEADS - 1, vslot).wait()

                    dot_u_mix_v(bs, t, hd, uslot, vslot, pbs, pt, v_init=False)

                    @pl.when(jnp.logical_and(t == 0, b >= 1))
                    def _():
                        out_copy(b - 1, 1 - bs).start()
            return c2

        lax.fori_loop(0, SC_TOKB, token, 0)
        return carry

    lax.fori_loop(0, nblk, block, 0)
    last = nblk * SC_TOKB * PK_HEADS - 1
    lbs = (nblk - 1) % 2
    v_copy(lbs, SC_TOKB - 1, PK_HEADS - 1, last % SC_V_SLOTS).wait()
    mix_v_only(last % SC_V_SLOTS, lbs, SC_TOKB - 1)
    out_copy(nblk - 1, lbs).start()
    for b in range(max(nblk - 2, 0), nblk):
        out_copy(b, b % 2).wait()


def _pack_bf16_pairs(v):
    n, d = v.shape
    bits = lax.bitcast_convert_type(v.astype(BF16), jnp.uint16).astype(jnp.uint32)
    bits = bits.reshape(n, d // (2 * SC_LANES), 2, SC_LANES)
    words = (bits[:, :, 1, :] << 16) | bits[:, :, 0, :]
    return lax.bitcast_convert_type(words.reshape(n, d // 2), I32)


def _pack_bf16_halves(v):
    n, d = v.shape
    bits = lax.bitcast_convert_type(v.astype(BF16), jnp.uint16).astype(jnp.uint32)
    words = (bits[:, d // 2:] << 16) | bits[:, :d // 2]
    return lax.bitcast_convert_type(words, F32)


def _peer_sc(idx, gate, h, u_tab, v_packed, n_tok):
    d = h.shape[1]
    assert n_tok % (SC_CORES * SC_SUBCORES * SC_TOKB) == 0 and d % (2 * SC_LANES) == 0
    mesh = plsc.VectorSubcoreMesh(core_axis_name="c", subcore_axis_name="s",
                                  num_cores=SC_CORES, num_subcores=SC_SUBCORES)
    return pl.kernel(
        functools.partial(_peer_sc_body, n_tok),
        out_type=jax.ShapeDtypeStruct((n_tok, d), F32),
        mesh=mesh,
        scratch_types=[
            pltpu.VMEM((2, SC_TOKB, N_SEL), I32),
            pltpu.VMEM((2, SC_TOKB, N_SEL), F32),
            pltpu.VMEM((2, SC_TOKB, d), F32),
            pltpu.VMEM((2, SC_TOKB, d), F32),
            pltpu.VMEM((SC_U_SLOTS, PK_TOPK, d), F32),
            pltpu.VMEM((SC_V_SLOTS, PK_TOPK, d // 2), I32),
            pltpu.VMEM((PK_TOPK, SC_LANES), F32),
            pltpu.VMEM((2 * SC_LANES,), F32),
            pltpu.SemaphoreType.DMA((SC_U_SLOTS,)),
            pltpu.SemaphoreType.DMA((SC_V_SLOTS,)),
            pltpu.SemaphoreType.DMA((2,)),
            pltpu.SemaphoreType.DMA((2,)),
        ],
        compiler_params=pltpu.CompilerParams(needs_layout_passes=False),
        name="peer_sc",
    )(idx, gate, h, u_tab, v_packed)


def _final_kernel(x_ref, p_ref, nw_ref, *rest):
    out_ref = rest[-1]
    out_ref[...] = _rms_rows(x_ref[...] + p_ref[...], nw_ref[...])


def _final_norm(x2d, peer_out, norm_w, y_full, row0, t_full, tm):
    t, d = peer_out.shape
    assert row0 % tm == 0 and t % tm == 0
    blk0 = row0 // tm
    row = pl.BlockSpec((tm, d), lambda i: (i, 0))
    in_specs = [row, row, pl.BlockSpec((1, d), lambda i: (0, 0))]
    operands = [x2d, peer_out, norm_w.reshape(1, d)]
    aliases = {}
    if y_full is not None:
        in_specs.append(pl.BlockSpec(memory_space=pl.ANY))
        operands.append(y_full)
        aliases = {3: 0}
    return pl.pallas_call(
        _final_kernel,
        grid=(t // tm,),
        in_specs=in_specs,
        out_specs=pl.BlockSpec((tm, d), lambda i: (i + blk0, 0)),
        out_shape=jax.ShapeDtypeStruct((t_full, d), F32),
        input_output_aliases=aliases,
        compiler_params=_cparams("parallel"),
        name="final_norm",
    )(*operands)


def _gelu_tanh(x):
    return 0.5 * x * (1.0 + jnp.tanh(0.7978845608028654 * (x + 0.044715 * x * x * x)))


def _peer_tc_kernel(idx_hbm, uv_hbm, h_ref, gate_ref, x_ref, nw_ref, y_any, out_ref,
                    idx_smem, rows, acc_ref, gate_t_ref, idx_sem, row_sem):
    del y_any
    tb, d = h_ref.shape
    blk = pl.program_id(0)

    idx_copy = pltpu.make_async_copy(idx_hbm.at[pl.ds(blk * (tb * N_SEL), tb * N_SEL)], idx_smem, idx_sem)
    idx_copy.start()
    idx_copy.wait()

    def start_rows(tok, slot):
        for e in range(N_SEL):
            pltpu.make_async_copy(uv_hbm.at[pl.ds(idx_smem[tok * N_SEL + e], 1)],
                                  rows.at[slot, pl.ds(e, 1)], row_sem.at[slot]).start()

    def wait_rows(slot):
        pltpu.make_async_copy(uv_hbm.at[pl.ds(0, N_SEL)], rows.at[slot], row_sem.at[slot]).wait()

    lane = lax.broadcasted_iota(I32, (N_SEL, tb), 1)
    gate_t_ref[...] = gate_ref[...].T

    for j in range(TC_ROW_SLOTS - 1):
        start_rows(j, j)

    def body(j, carry):
        slot = j % TC_ROW_SLOTS
        ahead = j + TC_ROW_SLOTS - 1

        @pl.when(ahead < tb)
        def _():
            start_rows(ahead, ahead % TC_ROW_SLOTS)

        wait_rows(slot)
        x_row = h_ref[pl.ds(j, 1), :]
        u = rows[slot, :, 0:d]
        a = jnp.sum(u * x_row, axis=1, keepdims=True)
        g = jnp.sum(jnp.where(lane == j, gate_t_ref[...], 0.0), axis=1, keepdims=True)
        coef = g * _gelu_tanh(a)
        w = lax.bitcast_convert_type(rows[slot, :, d:d + d // 2], I32)
        v_lo = lax.bitcast_convert_type(w << 16, F32)
        v_hi = lax.bitcast_convert_type(w & jnp.int32(-65536), F32)
        acc_ref[pl.ds(j, 1), 0:d // 2] = jnp.sum(coef * v_lo, axis=0, keepdims=True)
        acc_ref[pl.ds(j, 1), d // 2:d] = jnp.sum(coef * v_hi, axis=0, keepdims=True)
        return carry

    lax.fori_loop(0, tb, body, 0)
    out_ref[...] = _rms_rows(x_ref[...] + acc_ref[...], nw_ref[...])


def _peer_tc(idx_flat, uv, h, gate, x2d, norm_w, y_full, row0, t_full, tb):
    t, d = x2d.shape
    assert tb == N_SEL and t % tb == 0 and row0 % tb == 0 and y_full.shape == (t_full, d)
    blk0 = row0 // tb
    row = pl.BlockSpec((tb, d), lambda i: (i, 0))
    return pl.pallas_call(
        _peer_tc_kernel,
        grid=(t // tb,),
        in_specs=[pl.BlockSpec(memory_space=pl.ANY), pl.BlockSpec(memory_space=pl.ANY),
                  row, pl.BlockSpec((tb, N_SEL), lambda i: (i, 0)), row,
                  pl.BlockSpec((1, d), lambda i: (0, 0)), pl.BlockSpec(memory_space=pl.ANY)],
        out_specs=pl.BlockSpec((tb, d), lambda i: (i + blk0, 0)),
        out_shape=jax.ShapeDtypeStruct((t_full, d), F32),
        input_output_aliases={6: 0},
        scratch_shapes=[pltpu.SMEM((tb * N_SEL,), I32),
                        pltpu.VMEM((TC_ROW_SLOTS, N_SEL, d + d // 2), F32),
                        pltpu.VMEM((tb, d), F32),
                        pltpu.VMEM((N_SEL, tb), F32),
                        pltpu.SemaphoreType.DMA(()),
                        pltpu.SemaphoreType.DMA((TC_ROW_SLOTS,))],
        compiler_params=_cparams("arbitrary"),
        name="peer_tc",
    )(idx_flat, uv, h, gate, x2d, norm_w.reshape(1, d), y_full)


def _dense_stages(x, mem, p):
    bn, s, d = x.shape
    n_mem = mem.shape[1]
    x2d = x.reshape(bn * s, d)

    z = _rms_matmul(x2d, p["norm_mix_w"], p["w_in"], tm=256)
    hf, hb = _mlstm_scan(z, p["gate_b"], bn, s)
    x1 = _mixer_out(x2d, z, hf, hb, p["conv_w"], p["conv_b"], p["mlstm_norm_w"], p["w_out"], s, tm=256)

    kv = _rms_matmul(mem.reshape(bn * n_mem, d), p["norm_mem_w"], p["xa_wkv"], tm=256)
    x2 = _xattn(x1, kv, p["norm_xattn_w"], p["xa_wq"], p["xa_wo"], s, n_mem, tm=256)

    h3, eidx, gate = _peer_route(x2, p["norm_ffn_w"], p["peer_wq"], p["peer_keys1"], p["peer_keys2"], tm=256)
    return x2, h3, eidx, gate


def _layer(groups, p):
    t_lead = groups[0][0].shape[0] * groups[0][0].shape[1]
    assert 0 < TC_TOKENS < t_lead
    sc_pieces = [[] for _ in groups]
    handle = last_route = None
    tc_pieces = []
    for gi, (x, mem) in enumerate(groups):
        bn, s, d = x.shape
        rows = max(1, min(bn, PIECE_TOKENS // s))
        assert bn % rows == 0
        if handle is not None:
            x, _ = lax.optimization_barrier((x, handle))
        for i in range(0, bn, rows):
            x2, h3, eidx, gate = _dense_stages(x[i:i + rows], mem[i:i + rows], p)
            handle = eidx if i == 0 else handle
            last_route = eidx
            n_piece = rows * s
            n_sc = n_piece if gi > 0 else max(0, min(n_piece, t_lead - TC_TOKENS - i * s))
            if n_sc > 0:
                peer_out = _peer_sc(eidx, gate, h3, p["peer_u"], p["peer_v_packed"], n_sc)
                sc_pieces[gi].append((x2, peer_out, i * s))
            if n_sc < n_piece:
                tc_pieces.append((eidx[n_sc:], gate[n_sc:], h3[n_sc:], x2[n_sc:], i * s + n_sc))
    outs = []
    for gi, (x, _) in enumerate(groups):
        t_full = x.shape[0] * x.shape[1]
        y = None
        for x2, peer_out, row0 in sc_pieces[gi]:
            y = _final_norm(x2, peer_out, p["norm_final_w"], y, row0, t_full, tm=512)
        for eidx, gate, h3, x2, row0 in (tc_pieces if gi == 0 else []):
            eidx, _ = lax.optimization_barrier((eidx, last_route))
            y = _peer_tc(eidx.reshape(-1), p["peer_uv"], h3, gate, x2, p["norm_final_w"], y, row0, t_full, tb=N_SEL)
        outs.append(y.reshape(x.shape))
    return outs


def kernel(x_prompt, x_sample, mem_prompt, mem_sample, norm_mix_w, w_in, gate_b, conv_w, conv_b, mlstm_norm_w, w_out, norm_xattn_w, norm_mem_w, xa_wq, xa_wk, xa_wv, xa_wo, norm_ffn_w, peer_wq, peer_keys1, peer_keys2, peer_u, peer_v, norm_final_w):
    assert w_in.shape[0] == 1, "single-layer trunk"
    d = x_prompt.shape[-1]
    gate_pad = LANES - GATE_COLS
    p = {
        "norm_mix_w": norm_mix_w[0],
        "w_in": jnp.pad(w_in[0], ((0, 0), (0, gate_pad))).astype(BF16),
        "gate_b": jnp.pad(gate_b[0], (0, gate_pad)).reshape(1, LANES),
        "conv_w": conv_w[0], "conv_b": conv_b[0], "mlstm_norm_w": mlstm_norm_w[0],
        "w_out": w_out[0].astype(BF16),
        "norm_xattn_w": norm_xattn_w[0], "norm_mem_w": norm_mem_w[0],
        "xa_wq": xa_wq[0].astype(BF16),
        "xa_wkv": jnp.concatenate([xa_wk[0], xa_wv[0]], axis=1).astype(BF16),
        "xa_wo": xa_wo[0].astype(BF16),
        "norm_ffn_w": norm_ffn_w[0],
        "peer_wq": peer_wq[0].astype(BF16),
        "peer_keys1": peer_keys1[0].reshape(PK_HEADS * N_KEYS, -1).astype(BF16),
        "peer_keys2": peer_keys2[0].reshape(PK_HEADS * N_KEYS, -1).astype(BF16),
        "peer_u": peer_u[0], "peer_v_packed": _pack_bf16_pairs(peer_v[0]),
        "peer_uv": jnp.concatenate([peer_u[0], _pack_bf16_halves(peer_v[0])], axis=1),
        "norm_final_w": norm_final_w,
    }
    assert p["w_in"].shape[1] == Z_MAIN + LANES and d == 1024
    groups = {"prompt": (x_prompt, mem_prompt), "sample": (x_sample, mem_sample)}
    order = sorted(groups, key=lambda g: groups[g][0].shape[1])
    out = dict(zip(order, _layer([groups[g] for g in order], p)))
    return (out["prompt"], out["sample"])
```

```python
import functools

import jax
import jax.numpy as jnp
from jax import lax
from jax.experimental import pallas as pl
from jax.experimental.pallas import tpu as pltpu
from jax.experimental.pallas import tpu_sc as plsc

F32 = jnp.float32
BF16 = jnp.bfloat16
I32 = jnp.int32

EPS = 1e-6
M_INIT = -1e30
LANES = 128
SUBLANES = 8

CONV_W = 512
ML_HEADS = 4
ML_DH = 128
ML_W = ML_HEADS * ML_DH
CHUNK = 128
GATE_COLS = 4 * ML_HEADS
XA_HEADS = 4
N_KEYS = 128
PK_HEADS = 8
PK_TOPK = 16
N_SEL = PK_HEADS * PK_TOPK
VMEM_LIMIT = 56 * 1024 * 1024
SC_CORES = 2
SC_SUBCORES = 16
SC_LANES = 16
SC_TOKB = 4
SC_U_SLOTS = 4
SC_V_SLOTS = 4
SC_U_AHEAD = 3
SC_V_AHEAD = 2
PIECE_TOKENS = 4096
TC_TOKENS = 14336
TC_ROW_SLOTS = 4

ZB_XT, ZB_BG, ZB_CG, ZB_Q, ZB_K, ZB_V, ZB_O = range(7)
Z_MAIN = 7 * 512
ZB_GATES = Z_MAIN // LANES


def _cparams(*sem):
    return pltpu.CompilerParams(dimension_semantics=sem, vmem_limit_bytes=VMEM_LIMIT)


def _rms_rows(x, w):
    return x * lax.rsqrt(jnp.mean(x * x, axis=-1, keepdims=True) + EPS) * w


def _log_sigmoid(x):
    return jnp.minimum(x, 0.0) - jnp.log(1.0 + jnp.exp(-jnp.abs(x)))


def _dot(a, b):
    return jnp.dot(a.astype(BF16), b.astype(BF16), preferred_element_type=F32)


def _dot_nt(a, b):
    return lax.dot_general(a.astype(BF16), b.astype(BF16), (((1,), (1,)), ((), ())),
                           preferred_element_type=F32)


def _dot_tn(a, b):
    return lax.dot_general(a.astype(BF16), b.astype(BF16), (((0,), (0,)), ((), ())),
                           preferred_element_type=F32)


def _rms_matmul_kernel(x_ref, nw_ref, w_ref, o_ref):
    h = _rms_rows(x_ref[...], nw_ref[...])
    o_ref[...] = _dot(h, w_ref[...])


def _rms_matmul(x2d, norm_w, w_bf16, tm):
    t, d = x2d.shape
    n = w_bf16.shape[1]
    return pl.pallas_call(
        _rms_matmul_kernel,
        grid=(t // tm,),
        in_specs=[pl.BlockSpec((tm, d), lambda i: (i, 0)),
                  pl.BlockSpec((1, d), lambda i: (0, 0)),
                  pl.BlockSpec((d, n), lambda i: (0, 0))],
        out_specs=pl.BlockSpec((tm, n), lambda i: (i, 0)),
        out_shape=jax.ShapeDtypeStruct((t, n), F32),
        compiler_params=_cparams("parallel"),
        name="rms_matmul",
    )(x2d, norm_w.reshape(1, d), w_bf16)


def _mlstm_direction(d, q_ref, k_ref, v_ref, g_ref, gb_ref, h_ref, c_ref, n_ref, m_ref):
    L = CHUNK
    r = lax.broadcasted_iota(I32, (L, L), 0)
    c = lax.broadcasted_iota(I32, (L, L), 1)
    allowed = (c <= r) if d == 0 else (c >= r)
    tri = allowed.astype(F32)
    g_all = g_ref[...] + gb_ref[...]
    lf = _log_sigmoid(g_all)
    a_all = jnp.dot(tri, lf, precision=lax.Precision.HIGHEST, preferred_element_type=F32)
    a_all_t = a_all.T
    g_all_t = g_all.T
    end = L - 1 if d == 0 else 0
    for hd in range(ML_HEADS):
        ci = d * ML_HEADS + hd
        cf = 2 * ML_HEADS + d * ML_HEADS + hd
        st = d * ML_HEADS + hd
        sl = slice(hd * ML_DH, (hd + 1) * ML_DH)
        q = q_ref[:, sl] * (ML_DH ** -0.5)
        k = k_ref[:, sl]
        v = v_ref[:, sl]
        a_col = a_all[:, cf:cf + 1]
        a_row = a_all_t[cf:cf + 1, :]
        ig_col = g_all[:, ci:ci + 1]
        ig_row = g_all_t[ci:ci + 1, :]
        g_tot = a_col[end:end + 1, :]
        c_prev = c_ref[st]
        n_prev = n_ref[st]
        m_prev = m_ref[st]

        dlog = jnp.where(allowed, a_col - a_row + ig_row, -jnp.inf)
        m_inter = a_col + m_prev
        m_t = jnp.maximum(m_inter, jnp.max(dlog, axis=1, keepdims=True))
        s = jnp.exp(dlog - m_t) * _dot_nt(q, k)
        e_inter = jnp.exp(m_inter - m_t)
        num = _dot(s, v) + e_inter * _dot(q, c_prev)
        den = jnp.sum(s, axis=1, keepdims=True) + e_inter * jnp.sum(q * n_prev, axis=1, keepdims=True)
        h_ref[:, sl] = num / jnp.maximum(jnp.abs(den), jnp.exp(-m_t))

        w_col = g_tot - a_col + ig_col
        m_new = jnp.maximum(g_tot + m_prev, jnp.max(w_col, axis=0, keepdims=True))
        decay = jnp.exp(g_tot + m_prev - m_new)
        ksc = k * jnp.exp(w_col - m_new)
        c_ref[st] = decay * c_prev + _dot_tn(ksc, v)
        n_ref[st] = decay * n_prev + jnp.sum(ksc, axis=0, keepdims=True)
        m_ref[st] = m_new


def _mlstm_scan_kernel(qf, kf, vf, gf, qb, kb, vb, gbk, gb_ref, hf_ref, hb_ref, c_ref, n_ref, m_ref):
    @pl.when(pl.program_id(1) == 0)
    def _():
        c_ref[...] = jnp.zeros_like(c_ref)
        n_ref[...] = jnp.zeros_like(n_ref)
        m_ref[...] = jnp.full_like(m_ref, M_INIT)

    _mlstm_direction(0, qf, kf, vf, gf, gb_ref, hf_ref, c_ref, n_ref, m_ref)
    _mlstm_direction(1, qb, kb, vb, gbk, gb_ref, hb_ref, c_ref, n_ref, m_ref)


def _mlstm_scan(z, gate_b_pad, bn, s):
    nc = s // CHUNK
    t = bn * s

    def fwd(col):
        return lambda b, i: (b * nc + i, col)

    def bwd(col):
        return lambda b, i: (b * nc + nc - 1 - i, col)

    wide = lambda im: pl.BlockSpec((CHUNK, 512), im)
    gate = lambda im: pl.BlockSpec((CHUNK, LANES), im)
    n_state = 2 * ML_HEADS
    return pl.pallas_call(
        _mlstm_scan_kernel,
        grid=(bn, nc),
        in_specs=[wide(fwd(ZB_Q)), wide(fwd(ZB_K)), wide(fwd(ZB_V)), gate(fwd(ZB_GATES)),
                  wide(bwd(ZB_Q)), wide(bwd(ZB_K)), wide(bwd(ZB_V)), gate(bwd(ZB_GATES)),
                  pl.BlockSpec((1, LANES), lambda b, i: (0, 0))],
        out_specs=[wide(fwd(0)), wide(bwd(0))],
        out_shape=[jax.ShapeDtypeStruct((t, ML_W), F32), jax.ShapeDtypeStruct((t, ML_W), F32)],
        scratch_shapes=[pltpu.VMEM((n_state, ML_DH, ML_DH), F32),
                        pltpu.VMEM((n_state, 1, ML_DH), F32),
                        pltpu.VMEM((n_state, 1, 1), F32)],
        compiler_params=_cparams("parallel", "arbitrary"),
        name="mlstm_scan",
    )(z, z, z, z, z, z, z, z, gate_b_pad)


def _mixer_out_kernel(tiles_per_seq, x_ref, xt_ref, bg_ref, cg_ref, o_ref, hf_ref, hb_ref,
                      xt_p, cg_p, xt_n, cg_n, cw_ref, cb_ref, mw_ref, wo_ref, out_ref):
    tm = x_ref.shape[0]
    pos = pl.program_id(0) % tiles_per_seq
    u = cg_ref[...] * xt_ref[...]
    has_prev = (pos > 0).astype(F32)
    has_next = (pos < tiles_per_seq - 1).astype(F32)
    u_before = (cg_p[...] * xt_p[...])[SUBLANES - 1:SUBLANES, :] * has_prev
    u_after = (cg_n[...] * xt_n[...])[0:1, :] * has_next
    row = lax.broadcasted_iota(I32, u.shape, 0)
    u_prev = jnp.where(row == 0, u_before, pltpu.roll(u, 1, axis=0))
    u_next = jnp.where(row == tm - 1, u_after, pltpu.roll(u, tm - 1, axis=0))
    conv = cw_ref[0:1, :] * u_prev + cw_ref[1:2, :] * u + cw_ref[2:3, :] * u_next + cb_ref[...]
    y_conv = bg_ref[...] * conv

    hm = hf_ref[...] + hb_ref[...]
    og = jax.nn.sigmoid(o_ref[...])
    acc = x_ref[...] + _dot(y_conv, wo_ref[0:CONV_W, :])
    for hd in range(ML_HEADS):
        sl = slice(hd * ML_DH, (hd + 1) * ML_DH)
        y_ml = og[:, sl] * _rms_rows(hm[:, sl], mw_ref[:, sl])
        acc += _dot(y_ml, wo_ref[CONV_W + hd * ML_DH:CONV_W + (hd + 1) * ML_DH, :])
    out_ref[...] = acc


def _mixer_out(x2d, z, hf, hb, conv_w, conv_b, ml_norm_w, w_out_bf16, s, tm):
    t, d = x2d.shape
    tiles_per_seq = s // tm
    rb = tm // SUBLANES
    last_rb = t // SUBLANES - 1
    wide = lambda col: pl.BlockSpec((tm, 512), lambda i: (i, col))
    halo_prev = lambda col: pl.BlockSpec((SUBLANES, 512), lambda i: (jnp.maximum(i * rb - 1, 0), col))
    halo_next = lambda col: pl.BlockSpec((SUBLANES, 512), lambda i: (jnp.minimum((i + 1) * rb, last_rb), col))
    const = lambda shape: pl.BlockSpec(shape, lambda i: (0, 0))
    return pl.pallas_call(
        functools.partial(_mixer_out_kernel, tiles_per_seq),
        grid=(t // tm,),
        in_specs=[pl.BlockSpec((tm, d), lambda i: (i, 0)),
                  wide(ZB_XT), wide(ZB_BG), wide(ZB_CG), wide(ZB_O),
                  wide(0), wide(0),
                  halo_prev(ZB_XT), halo_prev(ZB_CG), halo_next(ZB_XT), halo_next(ZB_CG),
                  const((3, CONV_W)), const((1, CONV_W)), const((1, ML_W)), const((d, d))],
        out_specs=pl.BlockSpec((tm, d), lambda i: (i, 0)),
        out_shape=jax.ShapeDtypeStruct((t, d), F32),
        compiler_params=_cparams("parallel"),
        name="mixer_out",
    )(x2d, z, z, z, z, hf, hb, z, z, z, z, conv_w, conv_b.reshape(1, -1), ml_norm_w.reshape(1, -1), w_out_bf16)


def _xattn_kernel(x_ref, kv_ref, nw_ref, wq_ref, wo_ref, out_ref):
    x = x_ref[...]
    d = x.shape[1]
    dh = d // XA_HEADS
    q = _dot(_rms_rows(x, nw_ref[...]), wq_ref[...])
    acc = x
    for hd in range(XA_HEADS):
        sl = slice(hd * dh, (hd + 1) * dh)
        s = _dot_nt(q[:, sl], kv_ref[:, sl]) * (dh ** -0.5)
        s = s - jnp.max(s, axis=-1, keepdims=True)
        p = jnp.exp(s)
        p = p / jnp.sum(p, axis=-1, keepdims=True)
        o = _dot(p, kv_ref[:, d + hd * dh:d + (hd + 1) * dh])
        acc += _dot(o, wo_ref[sl, :])
    out_ref[...] = acc


def _xattn(x2d, kv, norm_w, wq_bf16, wo_bf16, s, n_mem, tm):
    t, d = x2d.shape
    tiles_per_seq = s // tm
    const = lambda shape: pl.BlockSpec(shape, lambda i: (0, 0))
    return pl.pallas_call(
        _xattn_kernel,
        grid=(t // tm,),
        in_specs=[pl.BlockSpec((tm, d), lambda i: (i, 0)),
                  pl.BlockSpec((n_mem, 2 * d), lambda i: (i // tiles_per_seq, 0)),
                  const((1, d)), const((d, d)), const((d, d))],
        out_specs=pl.BlockSpec((tm, d), lambda i: (i, 0)),
        out_shape=jax.ShapeDtypeStruct((t, d), F32),
        compiler_params=_cparams("parallel"),
        name="xattn",
    )(x2d, kv, norm_w.reshape(1, d), wq_bf16, wo_bf16)


def _top_rows(s, prio, payload, k):
    big = jnp.int32(2 ** 30)
    vals, pays = [], []
    for _ in range(k):
        m = jnp.max(s, axis=0, keepdims=True)
        pm = jnp.min(jnp.where(s == m, prio, big), axis=0, keepdims=True)
        sel = prio == pm
        vals.append(m)
        pays.append(pm if payload is None else jnp.sum(jnp.where(sel, payload, 0), axis=0, keepdims=True))
        s = jnp.where(sel, -jnp.inf, s)
    return vals, pays


def _peer_route_kernel(x_ref, nw_ref, wq_ref, k1_ref, k2_ref, h_ref, idx_ref, gate_ref):
    tm = x_ref.shape[0]
    k = PK_TOPK
    h = _rms_rows(x_ref[...], nw_ref[...])
    h_ref[...] = h
    q = _dot(h, wq_ref[...])
    key_iota = lax.broadcasted_iota(I32, (N_KEYS, tm), 0)
    r = lax.broadcasted_iota(I32, (80, 1), 0)
    prio = jnp.where(r < 16, r,
                     jnp.where(r < 72, (1 + ((r - 16) >> 3)) * k + ((r - 16) & 7), (r - 64) * k))
    idx_rows, gate_rows = [], []
    for hd in range(PK_HEADS):
        tops = []
        for half, kref in ((0, k1_ref), (1, k2_ref)):
            c0 = hd * 2 * N_KEYS + half * N_KEYS
            s_t = _dot_nt(kref[hd * N_KEYS:(hd + 1) * N_KEYS, :], q[:, c0:c0 + N_KEYS])
            vals, ids = _top_rows(s_t, key_iota, None, k)
            tops.append((jnp.concatenate(vals, axis=0), jnp.concatenate(ids, axis=0)))
        (v1, i1), (v2, i2) = tops
        cand_blocks = [v1[0:1] + v2]
        cidx_blocks = [i1[0:1] * N_KEYS + i2]
        for a in range(1, 8):
            cand_blocks.append(v1[a:a + 1] + v2[0:8])
            cidx_blocks.append(i1[a:a + 1] * N_KEYS + i2[0:8])
        cand_blocks.append(v1[8:16] + v2[0:1])
        cidx_blocks.append(i1[8:16] * N_KEYS + i2[0:1])
        cand = jnp.concatenate(cand_blocks, axis=0)
        cidx = jnp.concatenate(cidx_blocks, axis=0)
        sc, eidx = _top_rows(cand, prio, cidx, k)
        e = [jnp.exp(v - sc[0]) for v in sc]
        tot = functools.reduce(lambda a, b: a + b, e)
        gate_rows.extend([ek / tot for ek in e])
        idx_rows.extend(eidx)
    gate_ref[...] = jnp.concatenate(gate_rows, axis=0).T
    idx_ref[...] = jnp.concatenate(idx_rows, axis=0).T


def _peer_route(x2d, norm_w, wq_bf16, keys1_bf16, keys2_bf16, tm):
    t, d = x2d.shape
    const = lambda shape: pl.BlockSpec(shape, lambda i: (0, 0))
    return pl.pallas_call(
        _peer_route_kernel,
        grid=(t // tm,),
        in_specs=[pl.BlockSpec((tm, d), lambda i: (i, 0)), const((1, d)), const(wq_bf16.shape),
                  const(keys1_bf16.shape), const(keys2_bf16.shape)],
        out_specs=[pl.BlockSpec((tm, d), lambda i: (i, 0)),
                   pl.BlockSpec((tm, N_SEL), lambda i: (i, 0)),
                   pl.BlockSpec((tm, N_SEL), lambda i: (i, 0))],
        out_shape=[jax.ShapeDtypeStruct((t, d), F32),
                   jax.ShapeDtypeStruct((t, N_SEL), I32),
                   jax.ShapeDtypeStruct((t, N_SEL), F32)],
        compiler_params=_cparams("parallel"),
        name="peer_route",
    )(x2d, norm_w.reshape(1, d), wq_bf16, keys1_bf16, keys2_bf16)


def _peer_sc_body(n_tok, idx_hbm, gate_hbm, h_hbm, u_hbm, v_hbm, out_hbm,
                  idx_v, gate_v, x_v, out_v, u_buf, v_buf, tmp, cvec, sem_u, sem_v, sem_blk, sem_out):
    assert SC_U_AHEAD < SC_U_SLOTS and SC_V_AHEAD + 1 < SC_V_SLOTS
    assert SC_V_AHEAD <= SC_U_AHEAD < PK_HEADS
    d = h_hbm.shape[1]
    per_worker = n_tok // (SC_CORES * SC_SUBCORES)
    nblk = per_worker // SC_TOKB
    wid = lax.axis_index("s") * SC_CORES + lax.axis_index("c")
    ngrp = d // (2 * SC_LANES)
    lanes = lax.iota(I32, SC_LANES)
    zero = jnp.zeros((SC_LANES,), F32)

    def blk_copies(b, bs):
        rows = pl.ds(wid * per_worker + b * SC_TOKB, SC_TOKB)
        return (pltpu.make_async_copy(idx_hbm.at[rows], idx_v.at[bs], sem_blk.at[bs]),
                pltpu.make_async_copy(gate_hbm.at[rows], gate_v.at[bs], sem_blk.at[bs]),
                pltpu.make_async_copy(h_hbm.at[rows], x_v.at[bs], sem_blk.at[bs]))

    def out_copy(b, bs):
        rows = pl.ds(wid * per_worker + b * SC_TOKB, SC_TOKB)
        return pltpu.make_async_copy(out_v.at[bs], out_hbm.at[rows], sem_out.at[bs])

    def u_copy(bs, t, hd, slot):
        ids = idx_v.at[bs, t, pl.ds(hd * PK_TOPK, PK_TOPK)]
        return pltpu.make_async_copy(u_hbm.at[ids], u_buf.at[slot], sem_u.at[slot])

    def v_copy(bs, t, hd, slot):
        ids = idx_v.at[bs, t, pl.ds(hd * PK_TOPK, PK_TOPK)]
        return pltpu.make_async_copy(v_hbm.at[ids], v_buf.at[slot], sem_v.at[slot])

    def store_coef(accs, bs, t, hd):
        for e in range(PK_TOPK):
            tmp[e, :] = accs[e]
        a = zero
        for l in range(SC_LANES):
            a = a + plsc.load_gather(tmp, [lanes, (lanes + l) & (SC_LANES - 1)])
        z = 0.7978845608028654 * (a + 0.044715 * a * a * a)
        e2 = jnp.exp(-2.0 * jnp.abs(z))
        th = (1.0 - e2) / (1.0 + e2)
        th = jnp.where(z < 0.0, -th, th)
        cvec[pl.ds(SC_LANES, SC_LANES)] = gate_v[bs, t, pl.ds(hd * PK_TOPK, PK_TOPK)] * (0.5 * a * (1.0 + th))

    def load_coef():
        return [plsc.load_gather(cvec, [jnp.full((SC_LANES,), SC_LANES + e, I32)]) for e in range(PK_TOPK)]

    def mix_v(cb, vslot, j, o0, o1):
        for e in range(PK_TOPK):
            w = v_buf[vslot, e, pl.ds(j * SC_LANES, SC_LANES)]
            o0 = o0 + cb[e] * plsc.bitcast(w << 16, F32)
            o1 = o1 + cb[e] * plsc.bitcast(w & jnp.int32(-65536), F32)
        return o0, o1

    def dot_u_mix_v(bs, t, hd, uslot, vslot, pbs, pt, v_init):
        cb = load_coef()

        def step(j, accs):
            sl0 = pl.ds(j * 2 * SC_LANES, SC_LANES)
            sl1 = pl.ds(j * 2 * SC_LANES + SC_LANES, SC_LANES)
            x0 = x_v[bs, t, sl0]
            x1 = x_v[bs, t, sl1]
            new = tuple(accs[e] + u_buf[uslot, e, sl0] * x0 + u_buf[uslot, e, sl1] * x1 for e in range(PK_TOPK))
            o0, o1 = (zero, zero) if v_init else (out_v[pbs, pt, sl0], out_v[pbs, pt, sl1])
            o0, o1 = mix_v(cb, vslot, j, o0, o1)
            out_v[pbs, pt, sl0] = o0
            out_v[pbs, pt, sl1] = o1
            return new

        accs = plsc.parallel_loop(0, ngrp, carry=(zero,) * PK_TOPK)(step)
        store_coef(accs, bs, t, hd)

    def mix_v_only(vslot, pbs, pt):
        cb = load_coef()

        def step(j):
            sl0 = pl.ds(j * 2 * SC_LANES, SC_LANES)
            sl1 = pl.ds(j * 2 * SC_LANES + SC_LANES, SC_LANES)
            o0, o1 = mix_v(cb, vslot, j, out_v[pbs, pt, sl0], out_v[pbs, pt, sl1])
            out_v[pbs, pt, sl0] = o0
            out_v[pbs, pt, sl1] = o1

        plsc.parallel_loop(0, ngrp)(step)

    cvec[pl.ds(SC_LANES, SC_LANES)] = zero
    for e in range(PK_TOPK):
        def zero_v(j, e=e):
            v_buf[SC_V_SLOTS - 1, e, pl.ds(j * SC_LANES, SC_LANES)] = jnp.zeros((SC_LANES,), I32)
        plsc.parallel_loop(0, ngrp)(zero_v)

    def zero_out(j):
        out_v[1, SC_TOKB - 1, pl.ds(j * SC_LANES, SC_LANES)] = zero
    plsc.parallel_loop(0, d // SC_LANES)(zero_out)

    for cp in blk_copies(0, 0):
        cp.start()
    for cp in blk_copies(0, 0):
        cp.wait()
    for n in range(SC_U_AHEAD):
        u_copy(0, n // PK_HEADS, n % PK_HEADS, n % SC_U_SLOTS).start()
    for n in range(SC_V_AHEAD):
        v_copy(0, n // PK_HEADS, n % PK_HEADS, n % SC_V_SLOTS).start()

    def block(b, carry):
        bs = b % 2

        @pl.when(b + 1 < nblk)
        def _():
            for cp in blk_copies(b + 1, 1 - bs):
                cp.start()

        @pl.when(b >= 2)
        def _():
            out_copy(b - 2, bs).wait()

        def token(t, c2):
            g0 = (b * SC_TOKB + t) * PK_HEADS
            for hd in range(PK_HEADS):
                i = g0 + hd
                for mk, ahead, nslots in ((u_copy, SC_U_AHEAD, SC_U_SLOTS), (v_copy, SC_V_AHEAD, SC_V_SLOTS)):
                    slot_ahead = (i + ahead) % nslots
                    hn = hd + ahead
                    if hn < PK_HEADS:
                        mk(bs, t, hn, slot_ahead).start()
                    else:
                        hn -= PK_HEADS

                        @pl.when(t + 1 < SC_TOKB)
                        def _(mk=mk, hn=hn, slot_ahead=slot_ahead):
                            mk(bs, t + 1, hn, slot_ahead).start()

                        @pl.when(jnp.logical_and(t + 1 == SC_TOKB, b + 1 < nblk))
                        def _(mk=mk, hn=hn, slot_ahead=slot_ahead):
                            if mk is u_copy and hn == 0:
                                for cp in blk_copies(b + 1, 1 - bs):
                                    cp.wait()
                            mk(1 - bs, 0, hn, slot_ahead).start()
                uslot = i % SC_U_SLOTS
                vslot = (i + SC_V_SLOTS - 1) % SC_V_SLOTS
                u_copy(bs, t, hd, uslot).wait()
                if hd >= 1:
                    v_copy(bs, t, hd - 1, vslot).wait()
                    dot_u_mix_v(bs, t, hd, uslot, vslot, bs, t, v_init=(hd == 1))
                else:
                    pbs = jnp.where(t == 0, 1 - bs, bs)
                    pt = jnp.where(t == 0, SC_TOKB - 1, t - 1)

                    @pl.when(i >= 1)
                    def _():
                        v_copy(pbs, pt, PK_HEADS - 1, vslot).wait()

                    dot_u_mix_v(bs, t, hd, uslot, vslot, pbs, pt, v_init=False)

                    @pl.when(jnp.logical_and(t == 0, b >= 1))
                    def _():
                        out_copy(b - 1, 1 - bs).start()
            return c2

        lax.fori_loop(0, SC_TOKB, token, 0)
        return carry

    lax.fori_loop(0, nblk, block, 0)
    last = nblk * SC_TOKB * PK_HEADS - 1
    lbs = (nblk - 1) % 2
    v_copy(lbs, SC_TOKB - 1, PK_HEADS - 1, last % SC_V_SLOTS).wait()
    mix_v_only(last % SC_V_SLOTS, lbs, SC_TOKB - 1)
    out_copy(nblk - 1, lbs).start()
    for b in range(max(nblk - 2, 0), nblk):
        out_copy(b, b % 2).wait()


def _pack_bf16_pairs(v):
    n, d = v.shape
    bits = lax.bitcast_convert_type(v.astype(BF16), jnp.uint16).astype(jnp.uint32)
    bits = bits.reshape(n, d // (2 * SC_LANES), 2, SC_LANES)
    words = (bits[:, :, 1, :] << 16) | bits[:, :, 0, :]
    return lax.bitcast_convert_type(words.reshape(n, d // 2), I32)


def _pack_bf16_halves(v):
    n, d = v.shape
    bits = lax.bitcast_convert_type(v.astype(BF16), jnp.uint16).astype(jnp.uint32)
    words = (bits[:, d // 2:] << 16) | bits[:, :d // 2]
    return lax.bitcast_convert_type(words, F32)


def _peer_sc(idx, gate, h, u_tab, v_packed, n_tok):
    d = h.shape[1]
    assert n_tok % (SC_CORES * SC_SUBCORES * SC_TOKB) == 0 and d % (2 * SC_LANES) == 0
    mesh = plsc.VectorSubcoreMesh(core_axis_name="c", subcore_axis_name="s",
                                  num_cores=SC_CORES, num_subcores=SC_SUBCORES)
    return pl.kernel(
        functools.partial(_peer_sc_body, n_tok),
        out_type=jax.ShapeDtypeStruct((n_tok, d), F32),
        mesh=mesh,
        scratch_types=[
            pltpu.VMEM((2, SC_TOKB, N_SEL), I32),
            pltpu.VMEM((2, SC_TOKB, N_SEL), F32),
            pltpu.VMEM((2, SC_TOKB, d), F32),
            pltpu.VMEM((2, SC_TOKB, d), F32),
            pltpu.VMEM((SC_U_SLOTS, PK_TOPK, d), F32),
            pltpu.VMEM((SC_V_SLOTS, PK_TOPK, d // 2), I32),
            pltpu.VMEM((PK_TOPK, SC_LANES), F32),
            pltpu.VMEM((2 * SC_LANES,), F32),
            pltpu.SemaphoreType.DMA((SC_U_SLOTS,)),
            pltpu.SemaphoreType.DMA((SC_V_SLOTS,)),
            pltpu.SemaphoreType.DMA((2,)),
            pltpu.SemaphoreType.DMA((2,)),
        ],
        compiler_params=pltpu.CompilerParams(needs_layout_passes=False),
        name="peer_sc",
    )(idx, gate, h, u_tab, v_packed)


def _final_kernel(x_ref, p_ref, nw_ref, *rest):
    out_ref = rest[-1]
    out_ref[...] = _rms_rows(x_ref[...] + p_ref[...], nw_ref[...])


def _final_norm(x2d, peer_out, norm_w, y_full, row0, t_full, tm):
    t, d = peer_out.shape
    assert row0 % tm == 0 and t % tm == 0
    blk0 = row0 // tm
    row = pl.BlockSpec((tm, d), lambda i: (i, 0))
    in_specs = [row, row, pl.BlockSpec((1, d), lambda i: (0, 0))]
    operands = [x2d, peer_out, norm_w.reshape(1, d)]
    aliases = {}
    if y_full is not None:
        in_specs.append(pl.BlockSpec(memory_space=pl.ANY))
        operands.append(y_full)
        aliases = {3: 0}
    return pl.pallas_call(
        _final_kernel,
        grid=(t // tm,),
        in_specs=in_specs,
        out_specs=pl.BlockSpec((tm, d), lambda i: (i + blk0, 0)),
        out_shape=jax.ShapeDtypeStruct((t_full, d), F32),
        input_output_aliases=aliases,
        compiler_params=_cparams("parallel"),
        name="final_norm",
    )(*operands)


def _gelu_tanh(x):
    return 0.5 * x * (1.0 + jnp.tanh(0.7978845608028654 * (x + 0.044715 * x * x * x)))


def _peer_tc_kernel(idx_hbm, uv_hbm, h_ref, gate_ref, x_ref, nw_ref, y_any, out_ref,
                    idx_smem, rows, acc_ref, gate_t_ref, idx_sem, row_sem):
    del y_any
    tb, d = h_ref.shape
    blk = pl.program_id(0)

    idx_copy = pltpu.make_async_copy(idx_hbm.at[pl.ds(blk * (tb * N_SEL), tb * N_SEL)], idx_smem, idx_sem)
    idx_copy.start()
    idx_copy.wait()

    def start_rows(tok, slot):
        for e in range(N_SEL):
            pltpu.make_async_copy(uv_hbm.at[pl.ds(idx_smem[tok * N_SEL + e], 1)],
                                  rows.at[slot, pl.ds(e, 1)], row_sem.at[slot]).start()

    def wait_rows(slot):
        pltpu.make_async_copy(uv_hbm.at[pl.ds(0, N_SEL)], rows.at[slot], row_sem.at[slot]).wait()

    lane = lax.broadcasted_iota(I32, (N_SEL, tb), 1)
    gate_t_ref[...] = gate_ref[...].T

    for j in range(TC_ROW_SLOTS - 1):
        start_rows(j, j)

    def mix(j):
        slot = j % TC_ROW_SLOTS
        wait_rows(slot)
        x_row = h_ref[pl.ds(j, 1), :]
        u = rows[slot, :, 0:d]
        a = jnp.sum(u * x_row, axis=1, keepdims=True)
        g = jnp.sum(jnp.where(lane == j, gate_t_ref[...], 0.0), axis=1, keepdims=True)
        coef = g * _gelu_tanh(a)
        w = lax.bitcast_convert_type(rows[slot, :, d:d + d // 2], I32)
        v_lo = lax.bitcast_convert_type(w << 16, F32)
        v_hi = lax.bitcast_convert_type(w & jnp.int32(-65536), F32)
        acc_ref[pl.ds(j, 1), 0:d // 2] = jnp.sum(coef * v_lo, axis=0, keepdims=True)
        acc_ref[pl.ds(j, 1), d // 2:d] = jnp.sum(coef * v_hi, axis=0, keepdims=True)

    def body(j, carry):
        mix(j)
        ahead = j + TC_ROW_SLOTS - 1
        start_rows(ahead, ahead % TC_ROW_SLOTS)
        return carry

    lax.fori_loop(0, tb - (TC_ROW_SLOTS - 1), body, 0)
    for j in range(tb - (TC_ROW_SLOTS - 1), tb):
        mix(j)
    out_ref[...] = _rms_rows(x_ref[...] + acc_ref[...], nw_ref[...])


def _peer_tc(idx_flat, uv, h, gate, x2d, norm_w, y_full, row0, t_full, tb):
    t, d = x2d.shape
    assert tb == N_SEL and t % tb == 0 and row0 % tb == 0 and y_full.shape == (t_full, d)
    blk0 = row0 // tb
    row = pl.BlockSpec((tb, d), lambda i: (i, 0))
    return pl.pallas_call(
        _peer_tc_kernel,
        grid=(t // tb,),
        in_specs=[pl.BlockSpec(memory_space=pl.ANY), pl.BlockSpec(memory_space=pl.ANY),
                  row, pl.BlockSpec((tb, N_SEL), lambda i: (i, 0)), row,
                  pl.BlockSpec((1, d), lambda i: (0, 0)), pl.BlockSpec(memory_space=pl.ANY)],
        out_specs=pl.BlockSpec((tb, d), lambda i: (i + blk0, 0)),
        out_shape=jax.ShapeDtypeStruct((t_full, d), F32),
        input_output_aliases={6: 0},
        scratch_shapes=[pltpu.SMEM((tb * N_SEL,), I32),
                        pltpu.VMEM((TC_ROW_SLOTS, N_SEL, d + d // 2), F32),
                        pltpu.VMEM((tb, d), F32),
                        pltpu.VMEM((N_SEL, tb), F32),
                        pltpu.SemaphoreType.DMA(()),
                        pltpu.SemaphoreType.DMA((TC_ROW_SLOTS,))],
        compiler_params=_cparams("arbitrary"),
        name="peer_tc",
    )(idx_flat, uv, h, gate, x2d, norm_w.reshape(1, d), y_full)


def _dense_stages(x, mem, p):
    bn, s, d = x.shape
    n_mem = mem.shape[1]
    x2d = x.reshape(bn * s, d)

    z = _rms_matmul(x2d, p["norm_mix_w"], p["w_in"], tm=256)
    hf, hb = _mlstm_scan(z, p["gate_b"], bn, s)
    x1 = _mixer_out(x2d, z, hf, hb, p["conv_w"], p["conv_b"], p["mlstm_norm_w"], p["w_out"], s, tm=256)

    kv = _rms_matmul(mem.reshape(bn * n_mem, d), p["norm_mem_w"], p["xa_wkv"], tm=256)
    x2 = _xattn(x1, kv, p["norm_xattn_w"], p["xa_wq"], p["xa_wo"], s, n_mem, tm=256)

    h3, eidx, gate = _peer_route(x2, p["norm_ffn_w"], p["peer_wq"], p["peer_keys1"], p["peer_keys2"], tm=256)
    return x2, h3, eidx, gate


def _layer(groups, p):
    t_lead = groups[0][0].shape[0] * groups[0][0].shape[1]
    assert 0 < TC_TOKENS < t_lead
    sc_pieces = [[] for _ in groups]
    handle = last_route = None
    tc_pieces = []
    for gi, (x, mem) in enumerate(groups):
        bn, s, d = x.shape
        rows = max(1, min(bn, PIECE_TOKENS // s))
        assert bn % rows == 0
        if handle is not None:
            x, _ = lax.optimization_barrier((x, handle))
        for i in range(0, bn, rows):
            x2, h3, eidx, gate = _dense_stages(x[i:i + rows], mem[i:i + rows], p)
            handle = eidx if i == 0 else handle
            last_route = eidx
            n_piece = rows * s
            n_sc = n_piece if gi > 0 else max(0, min(n_piece, t_lead - TC_TOKENS - i * s))
            if n_sc > 0:
                peer_out = _peer_sc(eidx, gate, h3, p["peer_u"], p["peer_v_packed"], n_sc)
                sc_pieces[gi].append((x2, peer_out, i * s))
            if n_sc < n_piece:
                tc_pieces.append((eidx[n_sc:], gate[n_sc:], h3[n_sc:], x2[n_sc:], i * s + n_sc))
    outs = []
    for gi, (x, _) in enumerate(groups):
        t_full = x.shape[0] * x.shape[1]
        y = None
        for x2, peer_out, row0 in sc_pieces[gi]:
            y = _final_norm(x2, peer_out, p["norm_final_w"], y, row0, t_full, tm=512)
        for eidx, gate, h3, x2, row0 in (tc_pieces if gi == 0 else []):
            eidx, _ = lax.optimization_barrier((eidx, last_route))
            y = _peer_tc(eidx.reshape(-1), p["peer_uv"], h3, gate, x2, p["norm_final_w"], y, row0, t_full, tb=N_SEL)
        outs.append(y.reshape(x.shape))
    return outs


def kernel(x_prompt, x_sample, mem_prompt, mem_sample, norm_mix_w, w_in, gate_b, conv_w, conv_b, mlstm_norm_w, w_out, norm_xattn_w, norm_mem_w, xa_wq, xa_wk, xa_wv, xa_wo, norm_ffn_w, peer_wq, peer_keys1, peer_keys2, peer_u, peer_v, norm_final_w):
    assert w_in.shape[0] == 1, "single-layer trunk"
    d = x_prompt.shape[-1]
    gate_pad = LANES - GATE_COLS
    p = {
        "norm_mix_w": norm_mix_w[0],
        "w_in": jnp.pad(w_in[0], ((0, 0), (0, gate_pad))).astype(BF16),
        "gate_b": jnp.pad(gate_b[0], (0, gate_pad)).reshape(1, LANES),
        "conv_w": conv_w[0], "conv_b": conv_b[0], "mlstm_norm_w": mlstm_norm_w[0],
        "w_out": w_out[0].astype(BF16),
        "norm_xattn_w": norm_xattn_w[0], "norm_mem_w": norm_mem_w[0],
        "xa_wq": xa_wq[0].astype(BF16),
        "xa_wkv": jnp.concatenate([xa_wk[0], xa_wv[0]], axis=1).astype(BF16),
        "xa_wo": xa_wo[0].astype(BF16),
        "norm_ffn_w": norm_ffn_w[0],
        "peer_wq": peer_wq[0].astype(BF16),
        "peer_keys1": peer_keys1[0].reshape(PK_HEADS * N_KEYS, -1).astype(BF16),
        "peer_keys2": peer_keys2[0].reshape(PK_HEADS * N_KEYS, -1).astype(BF16),
        "peer_u": peer_u[0], "peer_v_packed": _pack_bf16_pairs(peer_v[0]),
        "peer_uv": jnp.concatenate([peer_u[0], _pack_bf16_halves(peer_v[0])], axis=1),
        "norm_final_w": norm_final_w,
    }
    assert p["w_in"].shape[1] == Z_MAIN + LANES and d == 1024
    groups = {"prompt": (x_prompt, mem_prompt), "sample": (x_sample, mem_sample)}
    order = sorted(groups, key=lambda g: groups[g][0].shape[1])
    out = dict(zip(order, _layer([groups[g] for g in order], p)))
    return (out["prompt"], out["sample"])
```
